```python
import math
import jax
import jax.numpy as jnp
from jax import lax
import numpy as np

D_MODEL = 1024
BATCH = 16
SEQ = 2048
DEPTH = 4

GRID_W = 64
CTX_LEN = 256
EPS = 1e-6
CONV_K = 3

DN_HEADS = 4
DN_DK = 128
DN_DV = 128
DN_DIRS = 2
DN_CHUNK = 64

ATT_HEADS = 8
ATT_KV_HEADS = 2
ATT_HD = 64
ATT_GROUPS = ATT_HEADS // ATT_KV_HEADS
Q_BLOCK = 128
ROPE_THETA = 10000.0
ROPE_AXIS_DIM = ATT_HD // 2

SC_DIM = D_MODEL

D_FF = 2816

N_EVEN = (DEPTH + 1) // 2
N_ODD = DEPTH // 2

DN_K_W = DN_HEADS * DN_DK
DN_V_W = DN_HEADS * DN_DV
DN_SCAL_W = DN_DIRS * DN_HEADS
ATT_Q_W = ATT_HEADS * ATT_HD
ATT_KV_W = ATT_KV_HEADS * ATT_HD
OFF_DN_K = 0
OFF_DN_V = OFF_DN_K + DN_K_W
OFF_DN_ALPHA = OFF_DN_V + DN_V_W
OFF_DN_BETA = OFF_DN_ALPHA + DN_SCAL_W
OFF_ATT_K = OFF_DN_BETA + DN_SCAL_W
OFF_ATT_V = OFF_ATT_K + ATT_KV_W
KV_COLS = OFF_ATT_V + ATT_KV_W
QO_DN_Q = 0
QO_DN_G = QO_DN_Q + DN_K_W
QO_ATT_Q = QO_DN_G + DN_V_W
Q_COLS = QO_ATT_Q + ATT_Q_W
PROJ_COLS = KV_COLS + Q_COLS
MIX_W = DN_V_W + ATT_Q_W

kernel_name = 'hybrid_deltanet_gqa_shortconv_dit_trunk'


def rms_norm(x, w):
    xf = x.astype(jnp.float32)
    y = xf * lax.rsqrt(jnp.mean(xf * xf, axis=-1, keepdims=True) + EPS)
    return (y * w.astype(jnp.float32)).astype(x.dtype)


def l2_norm(x):
    xf = x.astype(jnp.float32)
    return xf * lax.rsqrt(jnp.sum(xf * xf, axis=-1, keepdims=True) + EPS)


def modulate(xn, shift, scale):
    return xn * (1 + scale) + shift


def dwconv(x, w):
    k, ch = w.shape
    pad = (k - 1) // 2
    return lax.conv_general_dilated(
        x, w[:, None, :].astype(x.dtype), window_strides=(1,), padding=[(pad, pad)],
        dimension_numbers=('NWC', 'WIO', 'NWC'), feature_group_count=ch)


def axial_rope(n):
    rows = n // GRID_W
    r = jnp.repeat(jnp.arange(rows, dtype=jnp.float32), GRID_W)
    col = jnp.tile(jnp.arange(GRID_W, dtype=jnp.float32), rows)
    inv = ROPE_THETA ** (-jnp.arange(0, ROPE_AXIS_DIM, 2, dtype=jnp.float32) / ROPE_AXIS_DIM)
    ang = jnp.concatenate([r[:, None] * inv, col[:, None] * inv], axis=-1)
    return jnp.cos(ang), jnp.sin(ang)


def apply_rope(x, cos, sin):
    xf = x.astype(jnp.float32).reshape(x.shape[:-1] + (ATT_HD // 2, 2))
    x0, x1 = xf[..., 0], xf[..., 1]
    cs, sn = cos[None, :, None, :], sin[None, :, None, :]
    out = jnp.stack([x0 * cs - x1 * sn, x0 * sn + x1 * cs], axis=-1)
    return out.reshape(x.shape).astype(x.dtype)


def _to_chunks(a):
    b, t = a.shape[:2]
    a = a.reshape((b, t // DN_CHUNK, DN_CHUNK) + a.shape[2:])
    return jnp.moveaxis(jnp.moveaxis(a, 1, 0), 3, 2)


def _from_chunks(a):
    a = jnp.moveaxis(jnp.moveaxis(a, 3, 2), 0, 1)
    return a.reshape((a.shape[0], a.shape[1] * a.shape[2]) + a.shape[3:])


def _delta_step(s, k_i, u_i, w_i, g_i):
    v_new = u_i - jnp.einsum('bhck,bhkv->bhcv', w_i, s)
    g_last = g_i[..., -1]
    k_dec = k_i * jnp.exp(g_last[..., None] - g_i)[..., None]
    s_next = s * jnp.exp(g_last)[..., None, None] + jnp.einsum('bhck,bhcv->bhkv', k_dec, v_new)
    return v_new, s_next


def chunk_gated_delta(k, v, g, beta, s0, q=None):
    kc, vc, bc = _to_chunks(k), _to_chunks(v), _to_chunks(beta)
    gc = jnp.cumsum(_to_chunks(g), axis=-1)
    idx = jnp.arange(DN_CHUNK)
    incl = idx[:, None] >= idx[None, :]
    strict = idx[:, None] > idx[None, :]
    decay = jnp.where(incl, jnp.exp(jnp.where(incl, gc[..., :, None] - gc[..., None, :], 0.0)), 0.0)
    kb = kc * bc[..., None]
    lower = jnp.where(strict, jnp.einsum('nbhid,nbhjd->nbhij', kb, kc) * decay, 0.0)
    eye = jnp.eye(DN_CHUNK, dtype=jnp.float32)
    tmat = lax.linalg.triangular_solve(eye + lower, jnp.broadcast_to(eye, lower.shape),
                                       left_side=True, lower=True, unit_diagonal=True)
    u = jnp.einsum('nbhij,nbhjd->nbhid', tmat, vc * bc[..., None])
    w = jnp.einsum('nbhij,nbhjd->nbhid', tmat, kb * jnp.exp(gc)[..., None])
    if q is None:
        def state_step(s, inp):
            _, s_next = _delta_step(s, *inp)
            return s_next, None
        s_final, _ = lax.scan(state_step, s0, (kc, u, w, gc))
        return None, s_final
    qc = _to_chunks(q) * (DN_DK ** -0.5)
    intra = jnp.where(incl, jnp.einsum('nbhid,nbhjd->nbhij', qc, kc) * decay, 0.0)

    def out_step(s, inp):
        q_i, a_i, k_i, u_i, w_i, g_i = inp
        v_new, s_next = _delta_step(s, k_i, u_i, w_i, g_i)
        o = (jnp.einsum('bhck,bhkv->bhcv', q_i * jnp.exp(g_i)[..., None], s)
             + jnp.einsum('bhij,bhjv->bhiv', a_i, v_new))
        return s_next, o

    s_final, o = lax.scan(out_step, s0, (qc, intra, kc, u, w, gc))
    return _from_chunks(o), s_final


def run_direction(q, k, v, g, beta, s0, d):
    g_d, b_d = g[:, :, d], beta[:, :, d]
    if d == 1:
        k, v, g_d, b_d = (jnp.flip(a, axis=1) for a in (k, v, g_d, b_d))
        q = None if q is None else jnp.flip(q, axis=1)
    o, s = chunk_gated_delta(k, v, g_d, b_d, s0, q)
    if d == 1 and o is not None:
        o = jnp.flip(o, axis=1)
    return o, s


def dn_prepare(kv_side, q_side, conv_w, a_log, dt_bias):
    b, t = kv_side.shape[:2]
    kv = jax.nn.silu(dwconv(kv_side[..., OFF_DN_K:OFF_DN_ALPHA], conv_w[:, :DN_K_W + DN_V_W]))
    k = l2_norm(kv[..., :DN_K_W].reshape(b, t, DN_HEADS, DN_DK))
    v = kv[..., DN_K_W:].reshape(b, t, DN_HEADS, DN_DV).astype(jnp.float32)
    alpha = kv_side[..., OFF_DN_ALPHA:OFF_DN_BETA].astype(jnp.float32).reshape(b, t, DN_DIRS, DN_HEADS)
    g = -jnp.exp(a_log.astype(jnp.float32)) * jax.nn.softplus(alpha + dt_bias.astype(jnp.float32))
    beta = jax.nn.sigmoid(kv_side[..., OFF_DN_BETA:OFF_ATT_K].astype(jnp.float32).reshape(b, t, DN_DIRS, DN_HEADS))
    q = None
    if q_side is not None:
        qs = jax.nn.silu(dwconv(q_side[..., QO_DN_Q:QO_DN_G], conv_w[:, DN_K_W + DN_V_W:]))
        q = l2_norm(qs.reshape(b, t, DN_HEADS, DN_DK))
    return q, k, v, g, beta


def bidir_deltanet(lat_in, ctx_in):
    b = lat_in[1].shape[0]
    zero = jnp.zeros((b, DN_HEADS, DN_DK, DN_DV), jnp.float32)
    o_lat, o_ctx = [], []
    for d in range(DN_DIRS):
        oc, s_ctx = run_direction(*ctx_in, zero, d)
        ol, _ = run_direction(*lat_in, s_ctx, d)
        o_lat.append(ol)
        o_ctx.append(oc)
    o_lat = o_lat[0] + o_lat[1]
    o_ctx = None if o_ctx[0] is None else o_ctx[0] + o_ctx[1]
    return o_lat, o_ctx


def dn_output(o, gate, w_norm, dtype):
    b, t = o.shape[:2]
    z = gate.astype(jnp.float32).reshape(b, t, DN_HEADS, DN_DV)
    y = rms_norm(o, w_norm) * jax.nn.silu(z)
    return y.reshape(b, t, DN_V_W).astype(dtype)


def _heads(a, nh):
    return a.reshape(a.shape[:2] + (nh, ATT_HD))


def gqa_softmax(q, k, v):
    b, tq = q.shape[:2]
    qg = q.reshape(b, tq, ATT_KV_HEADS, ATT_GROUPS, ATT_HD)
    s = jnp.einsum('bqkgd,bskd->bkgqs', qg, k, preferred_element_type=jnp.float32) * (ATT_HD ** -0.5)
    p = jax.nn.softmax(s, axis=-1).astype(v.dtype)
    o = jnp.einsum('bkgqs,bskd->bqkgd', p, v)
    return o.reshape(b, tq, ATT_Q_W)


def blocked_attention(q, k, v):
    b, n = q.shape[:2]
    nb = n // Q_BLOCK
    qb = jnp.moveaxis(q.reshape(b, nb, Q_BLOCK, ATT_HEADS, ATT_HD), 1, 0)
    o = lax.map(lambda blk: gqa_softmax(blk, k, v), qb)
    return jnp.moveaxis(o, 0, 1).reshape(b, n, ATT_Q_W)


def hybrid_mixer(h, hc, w_in, w_out, conv_w, a_log, dt_bias, dn_norm, q_norm, k_norm, ctx_out):
    p = h @ w_in
    p_kv, p_q = p[..., :KV_COLS], p[..., KV_COLS:]
    if ctx_out:
        pc = hc @ w_in
        pc_kv, pc_q = pc[..., :KV_COLS], pc[..., KV_COLS:]
    else:
        pc_kv, pc_q = hc @ w_in[:, :KV_COLS], None
    lat_in = dn_prepare(p_kv, p_q, conv_w, a_log, dt_bias)
    ctx_in = dn_prepare(pc_kv, pc_q, conv_w, a_log, dt_bias)
    o_lat, o_ctx = bidir_deltanet(lat_in, ctx_in)
    a_lat = dn_output(o_lat, p_q[..., QO_DN_G:QO_ATT_Q], dn_norm, h.dtype)
    cos, sin = axial_rope(h.shape[1])
    q_lat = apply_rope(rms_norm(_heads(p_q[..., QO_ATT_Q:], ATT_HEADS), q_norm), cos, sin)
    k_lat = apply_rope(rms_norm(_heads(p_kv[..., OFF_ATT_K:OFF_ATT_V], ATT_KV_HEADS), k_norm), cos, sin)
    k_ctx = rms_norm(_heads(pc_kv[..., OFF_ATT_K:OFF_ATT_V], ATT_KV_HEADS), k_norm)
    v_ctx = _heads(pc_kv[..., OFF_ATT_V:], ATT_KV_HEADS)
    k_all = jnp.concatenate([k_ctx, k_lat], axis=1)
    v_all = jnp.concatenate([v_ctx, _heads(p_kv[..., OFF_ATT_V:], ATT_KV_HEADS)], axis=1)
    b_lat = blocked_attention(q_lat, k_all, v_all)
    y_lat = jnp.concatenate([a_lat, b_lat], axis=-1) @ w_out
    if not ctx_out:
        return y_lat, None
    a_ctx = dn_output(o_ctx, pc_q[..., QO_DN_G:QO_ATT_Q], dn_norm, hc.dtype)
    q_ctx = rms_norm(_heads(pc_q[..., QO_ATT_Q:], ATT_HEADS), q_norm)
    b_ctx = gqa_softmax(q_ctx, k_ctx, v_ctx)
    y_ctx = jnp.concatenate([a_ctx, b_ctx], axis=-1) @ w_out
    return y_lat, y_ctx


def short_conv_mixer(h, w_in, conv_w, w_out):
    gate_b, gate_c, xt = jnp.split(h @ w_in, 3, axis=-1)
    return (gate_b * dwconv(gate_c * xt, conv_w)) @ w_out


def conv_ffn(h, w_up, conv_w, w_down):
    a, v = jnp.split(h @ w_up, 2, axis=-1)
    return (jax.nn.silu(dwconv(a, conv_w)) * v) @ w_down


def setup_inputs(seed: int = 0) -> dict:
    key = jax.random.key(seed)
    ks = jax.random.split(key, 26)
    f32 = jnp.float32

    def nrm(k, shape, scale):
        return jax.random.normal(k, shape, f32) * scale

    def gain(k, shape):
        return 1.0 + 0.05 * jax.random.normal(k, shape, f32)

    dt = jnp.exp(jax.random.uniform(ks[14], (N_EVEN, DN_DIRS, DN_HEADS), f32,
                                    minval=math.log(1e-3), maxval=math.log(0.1)))
    return {
        'x': nrm(ks[0], (BATCH, SEQ, D_MODEL), 1.0),
        'c': nrm(ks[1], (BATCH, D_MODEL), 1.0),
        'ctx': nrm(ks[2], (BATCH, CTX_LEN, D_MODEL), 1.0),
        'c_ctx': nrm(ks[3], (D_MODEL,), 1.0),
        'mod_w': nrm(ks[4], (DEPTH, D_MODEL, 6 * D_MODEL), D_MODEL ** -0.5),
        'mod_b': nrm(ks[5], (DEPTH, 6 * D_MODEL), 0.02),
        'norm_mix_pre': gain(ks[6], (DEPTH, D_MODEL)),
        'norm_mix_post': gain(ks[7], (DEPTH, D_MODEL)),
        'norm_ffn_pre': gain(ks[8], (DEPTH, D_MODEL)),
        'norm_ffn_post': gain(ks[9], (DEPTH, D_MODEL)),
        'hyb_w_in': nrm(ks[10], (N_EVEN, D_MODEL, PROJ_COLS), D_MODEL ** -0.5),
        'hyb_w_out': nrm(ks[11], (N_EVEN, MIX_W, D_MODEL), MIX_W ** -0.5),
        'dn_conv_w': nrm(ks[12], (N_EVEN, CONV_K, DN_K_W + DN_V_W + DN_K_W), CONV_K ** -0.5),
        'dn_a_log': jnp.log(jax.random.uniform(ks[13], (N_EVEN, DN_DIRS, DN_HEADS), f32, minval=1.0, maxval=16.0)),
        'dn_dt_bias': dt + jnp.log(-jnp.expm1(-dt)),
        'dn_out_norm': gain(ks[15], (N_EVEN, DN_DV)),
        'att_q_norm': gain(ks[16], (N_EVEN, ATT_HD)),
        'att_k_norm': gain(ks[17], (N_EVEN, ATT_HD)),
        'sc_w_in': nrm(ks[18], (N_ODD, D_MODEL, 3 * SC_DIM), D_MODEL ** -0.5),
        'sc_conv_w': nrm(ks[19], (N_ODD, CONV_K, SC_DIM), CONV_K ** -0.5),
        'sc_w_out': nrm(ks[20], (N_ODD, SC_DIM, D_MODEL), SC_DIM ** -0.5),
        'ffn_w_up': nrm(ks[21], (DEPTH, D_MODEL, 2 * D_FF), D_MODEL ** -0.5),
        'ffn_conv_w': nrm(ks[22], (DEPTH, CONV_K, D_FF), CONV_K ** -0.5),
        'ffn_w_down': nrm(ks[23], (DEPTH, D_FF, D_MODEL), D_FF ** -0.5),
    }


def reference(x, c, ctx, c_ctx, mod_w, mod_b, norm_mix_pre, norm_mix_post, norm_ffn_pre, norm_ffn_post,
              hyb_w_in, hyb_w_out, dn_conv_w, dn_a_log, dn_dt_bias, dn_out_norm, att_q_norm, att_k_norm,
              sc_w_in, sc_conv_w, sc_w_out, ffn_w_up, ffn_conv_w, ffn_w_down):
    s_lat = jax.nn.silu(c)[:, None, :]
    s_ctx = jax.nn.silu(c_ctx)
    for layer in range(DEPTH):
        even = layer % 2 == 0
        ctx_live = any(j % 2 == 0 for j in range(layer + 1, DEPTH))
        mod = jnp.split(s_lat @ mod_w[layer] + mod_b[layer], 6, axis=-1)
        h = modulate(rms_norm(x, norm_mix_pre[layer]), mod[0], mod[1])
        if even or ctx_live:
            modc = jnp.split(s_ctx @ mod_w[layer] + mod_b[layer], 6, axis=-1)
            hc = modulate(rms_norm(ctx, norm_mix_pre[layer]), modc[0], modc[1])
        if even:
            e = layer // 2
            y, yc = hybrid_mixer(h, hc, hyb_w_in[e], hyb_w_out[e], dn_conv_w[e], dn_a_log[e], dn_dt_bias[e],
                                 dn_out_norm[e], att_q_norm[e], att_k_norm[e], ctx_live)
        else:
            o = layer // 2
            y = short_conv_mixer(h, sc_w_in[o], sc_conv_w[o], sc_w_out[o])
            yc = short_conv_mixer(hc, sc_w_in[o], sc_conv_w[o], sc_w_out[o]) if ctx_live else None
        x = x + mod[2] * rms_norm(y, norm_mix_post[layer])
        h = modulate(rms_norm(x, norm_ffn_pre[layer]), mod[3], mod[4])
        x = x + mod[5] * rms_norm(conv_ffn(h, ffn_w_up[layer], ffn_conv_w[layer], ffn_w_down[layer]),
                                  norm_ffn_post[layer])
        if ctx_live:
            ctx = ctx + modc[2] * rms_norm(yc, norm_mix_post[layer])
            hc = modulate(rms_norm(ctx, norm_ffn_pre[layer]), modc[3], modc[4])
            ctx = ctx + modc[5] * rms_norm(conv_ffn(hc, ffn_w_up[layer], ffn_conv_w[layer], ffn_w_down[layer]),
                                          norm_ffn_post[layer])
    return x
```

```python
import functools

import jax
import jax.numpy as jnp
import numpy as np
from jax import lax
from jax.experimental import pallas as pl
from jax.experimental.pallas import tpu as pltpu

F32 = jnp.float32
BF16 = jnp.bfloat16
EPS = 1e-6
HALO = 8
LANES = 128

D_MODEL = 1024
GRID_W = 64
ROPE_THETA = 10000.0
DN_HEADS = 4
DN_DK = 128
DN_DIRS = 2
DN_CHUNK = 64
ATT_HEADS = 8
ATT_KV_HEADS = 2
ATT_HD = 64
ATT_GROUPS = ATT_HEADS // ATT_KV_HEADS
D_FF = 2816
DN_W = DN_HEADS * DN_DK
ATT_Q_W = ATT_HEADS * ATT_HD
ATT_KV_W = ATT_KV_HEADS * ATT_HD

PC_K, PC_V, PC_Q, PC_Z, PC_AQ, PC_AK, PC_AV, PC_AB, PC_END = (
    0, 512, 1024, 1536, 2048, 2560, 2688, 2816, 2944)

VMEM_LIMIT = 56 * 1024 * 1024


def _resident(shape):
    nd = len(shape)
    return pl.BlockSpec(shape, lambda *_: (0,) * nd, pipeline_mode=pl.Buffered(1))


def _rms(x, w):
    return x * lax.rsqrt(jnp.mean(x * x, axis=-1, keepdims=True) + EPS) * w


def _silu(x):
    return x * (1.0 / (1.0 + jnp.exp(-x)))


def _dot(a, b):
    return jnp.dot(a, b, preferred_element_type=F32)


def _dot_nt(a, b):
    return lax.dot_general(a, b, (((1,), (1,)), ((), ())), preferred_element_type=F32)


def _dot_tn(a, b):
    return lax.dot_general(a, b, (((0,), (0,)), ((), ())), preferred_element_type=F32)


def _split3(x):
    hi = x.astype(BF16)
    r = x - hi.astype(F32)
    mid = r.astype(BF16)
    lo = (r - mid.astype(F32)).astype(BF16)
    return hi, mid, lo


def _valid_rows(tm):
    t = pl.program_id(1)
    nt = pl.num_programs(1)
    r = lax.broadcasted_iota(jnp.int32, (tm + 2 * HALO, 1), 0)
    lo_ok = jnp.logical_or(r >= HALO, t > 0)
    hi_ok = jnp.logical_or(r < tm + HALO, t < nt - 1)
    return jnp.logical_and(lo_ok, hi_ok)


def _conv3(cin, cw, tm):
    return (cw[0:1] * cin[HALO - 1:HALO - 1 + tm] + cw[1:2] * cin[HALO:HALO + tm]
            + cw[2:3] * cin[HALO + 1:HALO + 1 + tm])


def _halo_specs(tm, t_len, d):
    tb = tm // HALO
    nhb = t_len // HALO
    return [
        pl.BlockSpec((1, tm, d), lambda b, t: (b, t, 0)),
        pl.BlockSpec((1, HALO, d), lambda b, t: (b, jnp.maximum(t * tb - 1, 0), 0)),
        pl.BlockSpec((1, HALO, d), lambda b, t: (b, jnp.minimum((t + 1) * tb, nhb - 1), 0)),
    ]


def _mod_spec(mod):
    if mod.shape[0] > 1:
        return pl.BlockSpec((1, 6, mod.shape[2]), lambda b, t: (b, 0, 0))
    return pl.BlockSpec((1, 6, mod.shape[2]), lambda b, t: (0, 0, 0))


def _mod_kernel(s_ref, w_ref, b_ref, o_ref):
    s = _silu(s_ref[...])
    o_ref[0] = jnp.dot(s, w_ref[0], preferred_element_type=F32,
                       precision=lax.Precision.HIGHEST) + b_ref[0]


def _mod_call(s_pad, mod_w, mod_b):
    depth, d, n6 = mod_w.shape
    rows = s_pad.shape[0]
    nb = n6 // d
    return pl.pallas_call(
        _mod_kernel,
        out_shape=jax.ShapeDtypeStruct((depth, rows, n6), F32),
        grid=(depth, nb),
        in_specs=[
            pl.BlockSpec((rows, d), lambda l, j: (0, 0)),
            pl.BlockSpec((1, d, d), lambda l, j: (l, 0, j)),
            pl.BlockSpec((1, 1, d), lambda l, j: (l, 0, j)),
        ],
        out_specs=pl.BlockSpec((1, rows, d), lambda l, j: (l, 0, j)),
        compiler_params=pltpu.CompilerParams(
            dimension_semantics=("arbitrary", "arbitrary"), vmem_limit_bytes=VMEM_LIMIT),
        name="mod",
    )(s_pad, mod_w, mod_b.reshape(depth, 1, n6))


def _glu_kernel(xm_ref, xp_ref, xn_ref, mod_ref, npre_ref, npost_ref, win_ref, cw_ref, wout_ref,
                o_ref, acc_ref, *, mode, tm, nf, fc, rows):
    sh_i, sc_i, gt_i = rows
    x_main = xm_ref[0]
    x_ext = jnp.concatenate([xp_ref[0], x_main, xn_ref[0]], axis=0)
    shift = mod_ref[0, sh_i:sh_i + 1, :]
    scale = mod_ref[0, sc_i:sc_i + 1, :]
    gate = mod_ref[0, gt_i:gt_i + 1, :]
    hb = (_rms(x_ext, npre_ref[...]) * (1.0 + scale) + shift).astype(BF16)
    valid = _valid_rows(tm)
    acc_ref[...] = jnp.zeros_like(acc_ref)

    def body(j, carry):
        p = _dot(hb, win_ref[j])
        if mode == "ffn":
            cin = p[:, :fc]
        else:
            cin = p[:, fc:2 * fc] * p[:, 2 * fc:]
        cin = jnp.where(valid, cin, 0.0)
        conv = _conv3(cin, cw_ref[j], tm)
        if mode == "ffn":
            g = _silu(conv) * p[HALO:HALO + tm, fc:]
        else:
            g = p[HALO:HALO + tm, :fc] * conv
        acc_ref[...] += _dot(g.astype(BF16), wout_ref[j])
        return carry

    lax.fori_loop(0, nf, body, 0)
    o_ref[0] = x_main + gate * _rms(acc_ref[...], npost_ref[...])


def _glu_call(x, mod, npre, npost, win, cw, wout, *, mode, rows, tm):
    bsz, t_len, d = x.shape
    nf, _, _ = win.shape
    fc = wout.shape[1]
    tm = min(tm, t_len)
    kern = functools.partial(_glu_kernel, mode=mode, tm=tm, nf=nf, fc=fc, rows=rows)
    return pl.pallas_call(
        kern,
        out_shape=jax.ShapeDtypeStruct(x.shape, F32),
        grid=(bsz, t_len // tm),
        in_specs=_halo_specs(tm, t_len, d) + [
            _mod_spec(mod),
            pl.BlockSpec((1, d), lambda b, t: (0, 0)),
            pl.BlockSpec((1, d), lambda b, t: (0, 0)),
            _resident(win.shape), _resident(cw.shape), _resident(wout.shape),
        ],
        out_specs=pl.BlockSpec((1, tm, d), lambda b, t: (b, t, 0)),
        scratch_shapes=[pltpu.VMEM((tm, d), F32)],
        compiler_params=pltpu.CompilerParams(
            dimension_semantics=("parallel", "arbitrary"), vmem_limit_bytes=VMEM_LIMIT),
        name="glu_" + mode,
    )(x, x, x, mod, npre, npost, win, cw, wout)


def _chunk_cols(w, branches, fc):
    d = w.shape[0]
    f = w.shape[1] // branches
    w = w.reshape(d, branches, f // fc, fc)
    return jnp.transpose(w, (2, 0, 1, 3)).reshape(f // fc, d, branches * fc)


def _chunk_conv(cw, fc):
    k, f = cw.shape
    return jnp.transpose(cw.reshape(k, f // fc, fc), (1, 0, 2))


def _proj_kernel(*refs, tm, chunk, q_side):
    (xm_ref, xp_ref, xn_ref, mod_ref, npre_ref, w_ref, cw_ref, alog_ref, dtb_ref,
     qn_ref, kn_ref, cos_ref, sin_ref) = refs[:13]
    outs = refs[13:]
    k_o, v_o, gc_o, gcT_o, akT_o, av_o = outs[:6]
    if q_side:
        q_o, z_o, aq_o = outs[6:9]

    x_main = xm_ref[0]
    x_ext = jnp.concatenate([xp_ref[0], x_main, xn_ref[0]], axis=0)
    shift = mod_ref[0, 0:1, :]
    scale = mod_ref[0, 1:2, :]
    npre = npre_ref[...]
    hb_ext = (_rms(x_ext, npre) * (1.0 + scale) + shift).astype(BF16)
    hb = (_rms(x_main, npre) * (1.0 + scale) + shift).astype(BF16)
    valid = _valid_rows(tm)

    groups = [(PC_K, k_o, True, 1.0), (PC_V, v_o, False, 1.0)]
    if q_side:
        groups.append((PC_Q, q_o, True, DN_DK ** -0.5))
    for c0, o_ref, unit, mult in groups:
        p = jnp.where(valid, _dot(hb_ext, w_ref[:, c0:c0 + DN_W]), 0.0)
        a = _silu(_conv3(p, cw_ref[:, c0:c0 + DN_W], tm))
        if unit:
            for hd in range(DN_HEADS):
                seg = a[:, hd * DN_DK:(hd + 1) * DN_DK]
                inv = lax.rsqrt(jnp.sum(seg * seg, axis=-1, keepdims=True) + EPS)
                o_ref[0, :, hd * DN_DK:(hd + 1) * DN_DK] = seg * (inv * mult)
        else:
            o_ref[0] = a

    lane = lax.broadcasted_iota(jnp.int32, (1, LANES), 1)
    half = (lane % ATT_HD) < (ATT_HD // 2)
    cos_t = cos_ref[...]
    sin_t = sin_ref[...]
    r_i = lax.broadcasted_iota(jnp.int32, (LANES, LANES), 0)
    c_i = lax.broadcasted_iota(jnp.int32, (LANES, LANES), 1)
    head_mean = jnp.where((r_i // ATT_HD) == (c_i // ATT_HD), 1.0 / ATT_HD, 0.0).astype(BF16)

    def head_norm_rope(xa, wn):
        hi, mid, lo = _split3(xa * xa)
        ms = _dot(hi, head_mean) + _dot(mid, head_mean) + _dot(lo, head_mean)
        xa = xa * lax.rsqrt(ms + EPS) * wn
        swapped = jnp.where(half, pltpu.roll(xa, LANES - ATT_HD // 2, 1), pltpu.roll(xa, ATT_HD // 2, 1))
        return xa * cos_t + swapped * sin_t

    if q_side:
        z_o[0] = _dot(hb, w_ref[:, PC_Z:PC_AQ])
        pq = _dot(hb, w_ref[:, PC_AQ:PC_AK])
        for gq in range(ATT_Q_W // LANES):
            xa = head_norm_rope(pq[:, gq * LANES:(gq + 1) * LANES], qn_ref[...])
            aq_o[0, :, gq * LANES:(gq + 1) * LANES] = (xa * (ATT_HD ** -0.5)).astype(BF16)

    pk = _dot(hb, w_ref[:, PC_AK:PC_END])
    ak = head_norm_rope(pk[:, :LANES], kn_ref[...])
    akT_o[0] = ak.T.astype(BF16)
    av_o[0] = pk[:, LANES:2 * LANES].astype(BF16)

    pab = pk[:, 2 * LANES:]
    n_dh = DN_DIRS * DN_HEADS
    zab = pab + dtb_ref[...]
    softplus = jnp.maximum(zab, 0.0) + jnp.log(1.0 + jnp.exp(-jnp.abs(zab)))
    g = -jnp.exp(alog_ref[...]) * softplus
    beta = 1.0 / (1.0 + jnp.exp(-pab))
    g = jnp.where(lane < n_dh, g, 0.0)
    rr = lax.broadcasted_iota(jnp.int32, (tm, tm), 0)
    cc = lax.broadcasted_iota(jnp.int32, (tm, tm), 1)
    same = (rr // chunk) == (cc // chunk)
    low = jnp.where(jnp.logical_and(same, cc <= rr), 1.0, 0.0).astype(BF16)
    upp = jnp.where(jnp.logical_and(same, cc >= rr), 1.0, 0.0).astype(BF16)
    hi, mid, lo = _split3(g)
    pre = _dot(low, hi) + _dot(low, mid) + _dot(low, lo)
    suf = _dot(upp, hi) + _dot(upp, mid) + _dot(upp, lo)
    gc = jnp.where(lane < DN_HEADS, pre, jnp.where(lane < n_dh, suf, jnp.where(lane < 2 * n_dh, beta, 0.0)))
    gc_o[0] = gc
    gcT_o[0] = gc.T[0:2 * n_dh, :]


def _proj_call(x, mod, npre, w, cw, alog, dtb, qn, kn, cos_t, sin_t, *, q_side, tm, chunk):
    bsz, t_len, d = x.shape
    tm = min(tm, t_len)
    nt = t_len // tm
    f32_out = lambda n: jax.ShapeDtypeStruct((bsz, t_len, n), F32)
    row_spec = lambda n: pl.BlockSpec((1, tm, n), lambda b, t: (b, t, 0))
    out_shape = [f32_out(DN_W), f32_out(DN_W), f32_out(LANES),
                 jax.ShapeDtypeStruct((bsz, 2 * DN_DIRS * DN_HEADS, t_len), F32),
                 jax.ShapeDtypeStruct((bsz, ATT_KV_W, t_len), BF16),
                 jax.ShapeDtypeStruct((bsz, t_len, ATT_KV_W), BF16)]
    out_specs = [row_spec(DN_W), row_spec(DN_W), row_spec(LANES),
                 pl.BlockSpec((1, 2 * DN_DIRS * DN_HEADS, tm), lambda b, t: (b, 0, t)),
                 pl.BlockSpec((1, ATT_KV_W, tm), lambda b, t: (b, 0, t)),
                 row_spec(ATT_KV_W)]
    if q_side:
        out_shape += [f32_out(DN_W), f32_out(DN_W), jax.ShapeDtypeStruct((bsz, t_len, ATT_Q_W), BF16)]
        out_specs += [row_spec(DN_W), row_spec(DN_W), row_spec(ATT_Q_W)]
    vec = lambda n: pl.BlockSpec((1, n), lambda b, t: (0, 0))
    kern = functools.partial(_proj_kernel, tm=tm, chunk=chunk, q_side=q_side)
    return pl.pallas_call(
        kern,
        out_shape=out_shape,
        grid=(bsz, nt),
        in_specs=_halo_specs(tm, t_len, d) + [
            _mod_spec(mod), vec(d), _resident(w.shape), _resident(cw.shape),
            vec(LANES), vec(LANES), vec(LANES), vec(LANES),
            pl.BlockSpec((tm, LANES), lambda b, t: (t, 0)),
            pl.BlockSpec((tm, LANES), lambda b, t: (t, 0)),
        ],
        out_specs=out_specs,
        compiler_params=pltpu.CompilerParams(
            dimension_semantics=("parallel", "arbitrary"), vmem_limit_bytes=VMEM_LIMIT),
        name="proj_q" if q_side else "proj_kv",
    )(x, x, x, mod, npre, w, cw, alog, dtb, qn, kn, cos_t, sin_t)


def _dn_kernel(*refs, chunk, n_ctx, n_lat, ctx_q):
    c = chunk
    n_in_ctx = 5 if ctx_q else 4
    ctx_in = refs[:n_in_ctx]
    lat_in = refs[n_in_ctx:n_in_ctx + 5]
    n_out = 2 if ctx_q else 1
    outs = refs[n_in_ctx + 5:n_in_ctx + 5 + n_out]
    scr = refs[n_in_ctx + 5 + n_out:]
    u_s, w_s, kd_s, qg_s, o_s, in_s, eg_s = scr
    head = pl.program_id(1)

    ri = lax.broadcasted_iota(jnp.int32, (c, c), 0)
    ci = lax.broadcasted_iota(jnp.int32, (c, c), 1)
    eye = jnp.where(ri == ci, 1.0, 0.0)
    level_mask = []
    n = 1
    while n < c:
        level_mask.append(jnp.logical_and(ri // (2 * n) == ci // (2 * n), ri // n != ci // n))
        n *= 2
    lane =lax.broadcasted_iota(jnp.int32, (1, LANES), 1)
    sub = lax.broadcasted_iota(jnp.int32, (2 * DN_DIRS * DN_HEADS, 1), 0)
    n_dh = DN_DIRS * DN_HEADS

    def pick_col(x, idx):
        return jnp.sum(jnp.where(lane == idx, x, 0.0), axis=-1, keepdims=True)

    def prep(src, i, dst, has_q):
        if has_q:
            k_ref, v_ref, gc_ref, gcT_ref, q_ref = src
        else:
            k_ref, v_ref, gc_ref, gcT_ref = src
        r0 = pl.multiple_of(i * c, c)
        d0 = pl.multiple_of(dst * c, c)
        k = k_ref[0, pl.ds(r0, c), :]
        v = v_ref[0, pl.ds(r0, c), :]
        gcb = gc_ref[0, pl.ds(r0, c), :]
        gct = gcT_ref[0, i]
        kb16 = k.astype(BF16)
        if has_q:
            q = q_ref[0, pl.ds(r0, c), :]
        for d in range(DN_DIRS):
            col = d * DN_HEADS + head
            gc_col = pick_col(gcb, col)
            beta = pick_col(gcb, n_dh + col)
            gc_row = jnp.sum(jnp.where(sub == col, gct, 0.0), axis=0, keepdims=True)
            incl = (ci <= ri) if d == 0 else (ci >= ri)
            strict = (ci < ri) if d == 0 else (ci > ri)
            decay = jnp.where(incl, jnp.exp(jnp.where(incl, gc_col - gc_row, 0.0)), 0.0)
            kbeta = k * beta
            if has_q:
                lhs = jnp.concatenate([kbeta, q], axis=0).astype(BF16)
                kk = _dot_nt(lhs, kb16)
                low = jnp.where(strict, kk[:c] * decay, 0.0)
                in_s[d, pl.ds(d0, c), :] = jnp.where(incl, kk[c:] * decay, 0.0).astype(BF16)
            else:
                low = jnp.where(strict, _dot_nt(kbeta.astype(BF16), kb16) * decay, 0.0)
            tm_ = eye - jnp.where(level_mask[0], low, 0.0)
            for lm in level_mask[1:]:
                t16 = tm_.astype(BF16)
                et = _dot(jnp.where(lm, low, 0.0).astype(BF16), t16)
                tm_ = tm_ - _dot(t16, et.astype(BF16))
            egc = jnp.exp(gc_col)
            rhs = jnp.concatenate([v * beta, kbeta * egc], axis=1).astype(BF16)
            uw = _dot(tm_.astype(BF16), rhs)
            g_last = gc_col[c - 1:c] if d == 0 else gc_col[0:1]
            u_s[d, pl.ds(d0, c), :] = uw[:, :DN_DK]
            w_s[d, pl.ds(d0, c), :] = uw[:, DN_DK:].astype(BF16)
            kd_s[d, pl.ds(d0, c), :] = (k * jnp.exp(g_last - gc_col)).astype(BF16)
            eg_s[d, dst] = jnp.broadcast_to(jnp.exp(g_last), (HALO, LANES))
            if has_q:
                qg_s[d, pl.ds(d0, c), :] = (q * egc).astype(BF16)

    def prep_ctx(i, carry):
        prep(ctx_in, i, i, ctx_q)
        return carry

    def prep_lat(i, carry):
        prep(lat_in, i, n_ctx + i, True)
        return carry

    lax.fori_loop(0, n_ctx, prep_ctx, 0)
    lax.fori_loop(0, n_lat, prep_lat, 0)

    def scan_step(d, idx, s, has_q):
        d0 = pl.multiple_of(idx * c, c)
        s16 = s.astype(BF16)
        w = w_s[d, pl.ds(d0, c), :]
        u = u_s[d, pl.ds(d0, c), :]
        if has_q:
            ws = _dot(jnp.concatenate([w, qg_s[d, pl.ds(d0, c), :]], axis=0), s16)
            v_new = u - ws[:c]
            v16 = v_new.astype(BF16)
            o_s[d, pl.ds(d0, c), :] = ws[c:] + _dot(in_s[d, pl.ds(d0, c), :], v16)
        else:
            v16 = (u - _dot(w, s16)).astype(BF16)
        return s * eg_s[d, idx, 0:1, :] + _dot_tn(kd_s[d, pl.ds(d0, c), :], v16)

    def scan_ctx(i, carry):
        s0, s1 = carry
        return scan_step(0, i, s0, ctx_q), scan_step(1, n_ctx - 1 - i, s1, ctx_q)

    def scan_lat(i, carry):
        s0, s1 = carry
        return scan_step(0, n_ctx + i, s0, True), scan_step(1, n_ctx + n_lat - 1 - i, s1, True)

    zero = jnp.zeros((DN_DK, DN_DK), F32)
    carry = lax.fori_loop(0, n_ctx, scan_ctx, (zero, zero))
    lax.fori_loop(0, n_lat, scan_lat, carry)

    t_ctx = n_ctx * c
    outs[0][0] = o_s[0, t_ctx:, :] + o_s[1, t_ctx:, :]
    if ctx_q:
        outs[1][0] = o_s[0, :t_ctx, :] + o_s[1, :t_ctx, :]


def _dn_call(ctx_in, lat_in, *, chunk, ctx_q):
    k_l = lat_in[0]
    bsz, t_lat, _ = k_l.shape
    t_ctx = ctx_in[0].shape[1]
    n_ctx, n_lat = t_ctx // chunk, t_lat // chunk
    t_all = t_ctx + t_lat
    n_all = n_ctx + n_lat

    def specs(t_len, n, has_q):
        head_blk = pl.BlockSpec((1, t_len, DN_DK), lambda b, h: (b, 0, h))
        sp = [head_blk, head_blk,
              pl.BlockSpec((1, t_len, LANES), lambda b, h: (b, 0, 0)),
              pl.BlockSpec((1, n, 2 * DN_DIRS * DN_HEADS, chunk), lambda b, h: (b, 0, 0, 0))]
        if has_q:
            sp.append(head_blk)
        return sp

    out_shape = [jax.ShapeDtypeStruct((bsz, t_lat, DN_W), F32)]
    out_specs = [pl.BlockSpec((1, t_lat, DN_DK), lambda b, h: (b, 0, h))]
    if ctx_q:
        out_shape.append(jax.ShapeDtypeStruct((bsz, t_ctx, DN_W), F32))
        out_specs.append(pl.BlockSpec((1, t_ctx, DN_DK), lambda b, h: (b, 0, h)))
    nd = DN_DIRS
    scratch = [
        pltpu.VMEM((nd, t_all, DN_DK), F32),
        pltpu.VMEM((nd, t_all, DN_DK), BF16),
        pltpu.VMEM((nd, t_all, DN_DK), BF16),
        pltpu.VMEM((nd, t_all, DN_DK), BF16),
        pltpu.VMEM((nd, t_all, DN_DK), F32),
        pltpu.VMEM((nd, t_all, chunk), BF16),
        pltpu.VMEM((nd, n_all, HALO, LANES), F32),
    ]
    kern = functools.partial(_dn_kernel, chunk=chunk, n_ctx=n_ctx, n_lat=n_lat, ctx_q=ctx_q)
    return pl.pallas_call(
        kern,
        out_shape=out_shape,
        grid=(bsz, DN_HEADS),
        in_specs=specs(t_ctx, n_ctx, ctx_q) + specs(t_lat, n_lat, True),
        out_specs=out_specs,
        scratch_shapes=scratch,
        compiler_params=pltpu.CompilerParams(
            dimension_semantics=("parallel", "arbitrary"), vmem_limit_bytes=VMEM_LIMIT),
        name="dn_q" if ctx_q else "dn_kv",
    )(*ctx_in, *lat_in)


def _attn_kernel(*refs, n_sets):
    q_ref = refs[0]
    kv = refs[1:1 + 2 * n_sets]
    o_ref = refs[1 + 2 * n_sets]
    for g in range(ATT_KV_HEADS):
        for hh in range(ATT_GROUPS):
            h0 = (g * ATT_GROUPS + hh) * ATT_HD
            q = q_ref[0, :, h0:h0 + ATT_HD]
            scores = [_dot(q, kv[2 * i][0, g * ATT_HD:(g + 1) * ATT_HD, :]) for i in range(n_sets)]
            m = scores[0].max(axis=-1, keepdims=True)
            for s in scores[1:]:
                m = jnp.maximum(m, s.max(axis=-1, keepdims=True))
            den = 0.0
            acc = 0.0
            for i, s in enumerate(scores):
                p = jnp.exp(s - m)
                den = den + jnp.sum(p, axis=-1, keepdims=True)
                acc = acc + _dot(p.astype(BF16), kv[2 * i + 1][0])
            o_ref[0, :, h0:h0 + ATT_HD] = (acc[:, g * ATT_HD:(g + 1) * ATT_HD] / den).astype(BF16)


def _attn_call(q, kv_sets, *, tq):
    bsz, t_len, _ = q.shape
    tq = min(tq, t_len)
    in_specs = [pl.BlockSpec((1, tq, ATT_Q_W), lambda b, t: (b, t, 0))]
    args = [q]
    for k_t, v in kv_sets:
        s_len = v.shape[1]
        in_specs += [pl.BlockSpec((1, ATT_KV_W, s_len), lambda b, t: (b, 0, 0)),
                     pl.BlockSpec((1, s_len, ATT_KV_W), lambda b, t: (b, 0, 0))]
        args += [k_t, v]
    return pl.pallas_call(
        functools.partial(_attn_kernel, n_sets=len(kv_sets)),
        out_shape=jax.ShapeDtypeStruct((bsz, t_len, ATT_Q_W), BF16),
        grid=(bsz, t_len // tq),
        in_specs=in_specs,
        out_specs=pl.BlockSpec((1, tq, ATT_Q_W), lambda b, t: (b, t, 0)),
        compiler_params=pltpu.CompilerParams(
            dimension_semantics=("parallel", "arbitrary"), vmem_limit_bytes=VMEM_LIMIT),
        name="attn%d" % len(kv_sets),
    )(*args)


def _mixout_kernel(x_ref, o_ref, z_ref, b_ref, mod_ref, dnn_ref, npost_ref, w_ref, out_ref):
    gate = mod_ref[0, 2:3, :]
    o = o_ref[0]
    z = z_ref[0]
    parts = []
    for hd in range(DN_HEADS):
        seg = o[:, hd * DN_DK:(hd + 1) * DN_DK]
        parts.append(_rms(seg, dnn_ref[...]) * _silu(z[:, hd * DN_DK:(hd + 1) * DN_DK]))
    a = jnp.concatenate(parts, axis=1).astype(BF16)
    y = _dot(a, w_ref[0:DN_W, :]) + _dot(b_ref[0], w_ref[DN_W:, :])
    out_ref[0] = x_ref[0] + gate * _rms(y, npost_ref[...])


def _mixout_call(x, o_dn, z, b_att, mod, dnn, npost, w_out, *, tm):
    bsz, t_len, d = x.shape
    tm = min(tm, t_len)
    row = lambda n: pl.BlockSpec((1, tm, n), lambda b, t: (b, t, 0))
    return pl.pallas_call(
        _mixout_kernel,
        out_shape=jax.ShapeDtypeStruct(x.shape, F32),
        grid=(bsz, t_len // tm),
        in_specs=[row(d), row(DN_W), row(DN_W), row(ATT_Q_W), _mod_spec(mod),
                  pl.BlockSpec((1, DN_DK), lambda b, t: (0, 0)),
                  pl.BlockSpec((1, d), lambda b, t: (0, 0)),
                  _resident(w_out.shape)],
        out_specs=row(d),
        compiler_params=pltpu.CompilerParams(
            dimension_semantics=("parallel", "arbitrary"), vmem_limit_bytes=VMEM_LIMIT),
        name="mixout",
    )(x, o_dn, z, b_att, mod, dnn, npost, w_out)


def _rope_tables(n):
    half = ATT_HD // 2
    r = (jnp.arange(n) // GRID_W).astype(F32)
    col = (jnp.arange(n) % GRID_W).astype(F32)
    inv = ROPE_THETA ** (-jnp.arange(0, half, 2, dtype=F32) / half)
    ang = jnp.concatenate([r[:, None] * inv, col[:, None] * inv], axis=-1)
    cos, sin = jnp.cos(ang), jnp.sin(ang)
    cos_t = jnp.tile(jnp.concatenate([cos, cos], axis=-1), (1, LANES // ATT_HD))
    sin_t = jnp.tile(jnp.concatenate([-sin, sin], axis=-1), (1, LANES // ATT_HD))
    return cos_t, sin_t


def _even_weights(w_in, q_norm, k_norm, a_log, dt_bias):
    kv_cols = 2 * DN_W + 2 * DN_DIRS * DN_HEADS + 2 * ATT_KV_W
    perm = np.concatenate([np.arange(0, ATT_HD, 2), np.arange(1, ATT_HD, 2)])
    off_ab = 2 * DN_W
    off_ak = off_ab + 2 * DN_DIRS * DN_HEADS
    off_av = off_ak + ATT_KV_W
    off_q = kv_cols
    off_z = off_q + DN_W
    off_aq = off_z + DN_W
    aq_idx = np.concatenate([off_aq + h * ATT_HD + perm for h in range(ATT_HEADS)])
    ak_idx = np.concatenate([off_ak + h * ATT_HD + perm for h in range(ATT_KV_HEADS)])
    cols = np.concatenate([
        np.arange(0, 2 * DN_W), np.arange(off_q, off_q + DN_W), np.arange(off_z, off_z + DN_W),
        aq_idx, ak_idx, np.arange(off_av, off_av + ATT_KV_W), np.arange(off_ab, off_ab + 2 * DN_DIRS * DN_HEADS)])
    w = jnp.take(w_in, jnp.asarray(cols), axis=1)
    w = jnp.pad(w, ((0, 0), (0, PC_END - w.shape[1]))).astype(BF16)
    tile2 = lambda v: jnp.tile(v[perm], LANES // ATT_HD).reshape(1, LANES)
    pad = lambda v: jnp.pad(v.reshape(-1), (0, LANES - v.size)).reshape(1, LANES)
    return w, tile2(q_norm), tile2(k_norm), pad(a_log), pad(dt_bias)


def _chunked_t(gc_t, chunk):
    bsz, r, t_len = gc_t.shape
    return jnp.transpose(gc_t.reshape(bsz, r, t_len // chunk, chunk), (0, 2, 1, 3))


def kernel(x, c, ctx, c_ctx, mod_w, mod_b, norm_mix_pre, norm_mix_post, norm_ffn_pre, norm_ffn_post,
           hyb_w_in, hyb_w_out, dn_conv_w, dn_a_log, dn_dt_bias, dn_out_norm, att_q_norm, att_k_norm,
           sc_w_in, sc_conv_w, sc_w_out, ffn_w_up, ffn_conv_w, ffn_w_down):
    bsz, seq, d = x.shape
    depth = mod_w.shape[0]
    ctx_len = ctx.shape[1]
    fc = 256
    tm_glu, tm_proj, tm_out, tq = 512, 256, 512, 256

    rows = ((bsz + 1 + HALO - 1) // HALO) * HALO
    s_pad = jnp.zeros((rows, d), F32).at[:bsz].set(c).at[bsz].set(c_ctx)
    mod_all = _mod_call(s_pad, mod_w, mod_b).reshape(depth, rows, 6, d)

    cos_l, sin_l = _rope_tables(seq)
    cos_c, sin_c = jnp.ones((ctx_len, LANES), F32), jnp.zeros((ctx_len, LANES), F32)
    vec = lambda v: v.reshape(1, -1)

    for layer in range(depth):
        even = layer % 2 == 0
        ctx_live = any(j % 2 == 0 for j in range(layer + 1, depth))
        mod_l = mod_all[layer, :bsz]
        mod_c = mod_all[layer, bsz:bsz + 1]
        npre, npost = vec(norm_mix_pre[layer]), vec(norm_mix_post[layer])
        if even:
            e = layer // 2
            w, qn, kn, alog, dtb = _even_weights(hyb_w_in[e], att_q_norm[e], att_k_norm[e],
                                                 dn_a_log[e], dn_dt_bias[e])
            cw = dn_conv_w[e]
            w_out = hyb_w_out[e].astype(BF16)
            dnn = vec(dn_out_norm[e])
            proj = functools.partial(_proj_call, npre=npre, w=w, cw=cw, alog=alog, dtb=dtb, qn=qn, kn=kn,
                                     tm=tm_proj, chunk=DN_CHUNK)
            pl_ = proj(x, mod_l, cos_t=cos_l, sin_t=sin_l, q_side=True)
            pc_ = proj(ctx, mod_c, cos_t=cos_c, sin_t=sin_c, q_side=ctx_live)
            k_l, v_l, gc_l, gct_l, akt_l, av_l, q_l, z_l, aq_l = pl_
            k_c, v_c, gc_c, gct_c, akt_c, av_c = pc_[:6]
            lat_in = (k_l, v_l, gc_l, _chunked_t(gct_l, DN_CHUNK), q_l)
            ctx_in = (k_c, v_c, gc_c, _chunked_t(gct_c, DN_CHUNK)) + ((pc_[6],) if ctx_live else ())
            o_dn = _dn_call(ctx_in, lat_in, chunk=DN_CHUNK, ctx_q=ctx_live)
            b_lat = _attn_call(aq_l, [(akt_c, av_c), (akt_l, av_l)], tq=tq)
            x_new = _mixout_call(x, o_dn[0], z_l, b_lat, mod_l, dnn, npost, w_out, tm=tm_out)
            if ctx_live:
                b_ctx = _attn_call(pc_[8], [(akt_c, av_c)], tq=tq)
                ctx = _mixout_call(ctx, o_dn[1], pc_[7], b_ctx, mod_c, dnn, npost, w_out, tm=tm_out)
            x = x_new
        else:
            o = layer // 2
            win = _chunk_cols(sc_w_in[o], 3, fc).astype(BF16)
            cwc = _chunk_conv(sc_conv_w[o], fc)
            wout = sc_w_out[o].reshape(-1, fc, d).astype(BF16)
            x = _glu_call(x, mod_l, npre, npost, win, cwc, wout, mode="sc", rows=(0, 1, 2), tm=tm_glu)
            if ctx_live:
                ctx = _glu_call(ctx, mod_c, npre, npost, win, cwc, wout, mode="sc", rows=(0, 1, 2), tm=tm_glu)
        win = _chunk_cols(ffn_w_up[layer], 2, fc).astype(BF16)
        cwc = _chunk_conv(ffn_conv_w[layer], fc)
        wout = ffn_w_down[layer].reshape(-1, fc, d).astype(BF16)
        npre, npost = vec(norm_ffn_pre[layer]), vec(norm_ffn_post[layer])
        x = _glu_call(x, mod_l, npre, npost, win, cwc, wout, mode="ffn", rows=(3, 4, 5), tm=tm_glu)
        if ctx_live:
            ctx = _glu_call(ctx, mod_c, npre, npost, win, cwc, wout, mode="ffn", rows=(3, 4, 5), tm=tm_glu)
    return x
```

```python
import functools

import jax
import jax.numpy as jnp
import numpy as np
from jax import lax
from jax.experimental import pallas as pl
from jax.experimental.pallas import tpu as pltpu

F32 = jnp.float32
BF16 = jnp.bfloat16
EPS = 1e-6
HALO = 8
LANES = 128

D_MODEL = 1024
GRID_W = 64
ROPE_THETA = 10000.0
DN_HEADS = 4
DN_DK = 128
DN_DIRS = 2
DN_CHUNK = 64
DN_SUPER = 256
DN_HEADS_PER_STEP = 2
ATT_HEADS = 8
ATT_KV_HEADS = 2
ATT_HD = 64
ATT_GROUPS = ATT_HEADS // ATT_KV_HEADS
D_FF = 2816
DN_W = DN_HEADS * DN_DK
ATT_Q_W = ATT_HEADS * ATT_HD
ATT_KV_W = ATT_KV_HEADS * ATT_HD

PC_K, PC_V, PC_Q, PC_Z, PC_AQ, PC_AK, PC_AV, PC_AB, PC_END = (
    0, 512, 1024, 1536, 2048, 2560, 2688, 2816, 2944)

VMEM_LIMIT = 56 * 1024 * 1024


def _resident(shape):
    nd = len(shape)
    return pl.BlockSpec(shape, lambda *_: (0,) * nd, pipeline_mode=pl.Buffered(1))


def _rms(x, w):
    return x * lax.rsqrt(jnp.mean(x * x, axis=-1, keepdims=True) + EPS) * w


def _silu(x):
    return x * (1.0 / (1.0 + jnp.exp(-x)))


def _dot(a, b):
    return jnp.dot(a, b, preferred_element_type=F32)


def _dot_nt(a, b):
    return lax.dot_general(a, b, (((1,), (1,)), ((), ())), preferred_element_type=F32)


def _dot_tn(a, b):
    return lax.dot_general(a, b, (((0,), (0,)), ((), ())), preferred_element_type=F32)


def _split3(x):
    hi = x.astype(BF16)
    r = x - hi.astype(F32)
    mid = r.astype(BF16)
    lo = (r - mid.astype(F32)).astype(BF16)
    return hi, mid, lo


def _valid_rows(tm):
    t = pl.program_id(1)
    nt = pl.num_programs(1)
    r = lax.broadcasted_iota(jnp.int32, (tm + 2 * HALO, 1), 0)
    lo_ok = jnp.logical_or(r >= HALO, t > 0)
    hi_ok = jnp.logical_or(r < tm + HALO, t < nt - 1)
    return jnp.logical_and(lo_ok, hi_ok)


def _conv3(cin, cw, tm):
    return (cw[0:1] * cin[HALO - 1:HALO - 1 + tm] + cw[1:2] * cin[HALO:HALO + tm]
            + cw[2:3] * cin[HALO + 1:HALO + 1 + tm])


def _halo_specs(tm, t_len, d):
    tb = tm // HALO
    nhb = t_len // HALO
    return [
        pl.BlockSpec((1, tm, d), lambda b, t: (b, t, 0)),
        pl.BlockSpec((1, HALO, d), lambda b, t: (b, jnp.maximum(t * tb - 1, 0), 0)),
        pl.BlockSpec((1, HALO, d), lambda b, t: (b, jnp.minimum((t + 1) * tb, nhb - 1), 0)),
    ]


def _mod_spec(mod):
    if mod.shape[0] > 1:
        return pl.BlockSpec((1, 6, mod.shape[2]), lambda b, t: (b, 0, 0))
    return pl.BlockSpec((1, 6, mod.shape[2]), lambda b, t: (0, 0, 0))


def _mod_kernel(s_ref, w_ref, b_ref, o_ref):
    s = _silu(s_ref[...])
    o_ref[0] = jnp.dot(s, w_ref[0], preferred_element_type=F32,
                       precision=lax.Precision.HIGHEST) + b_ref[0]


def _mod_call(s_pad, mod_w, mod_b):
    depth, d, n6 = mod_w.shape
    rows = s_pad.shape[0]
    nb = n6 // d
    return pl.pallas_call(
        _mod_kernel,
        out_shape=jax.ShapeDtypeStruct((depth, rows, n6), F32),
        grid=(depth, nb),
        in_specs=[
            pl.BlockSpec((rows, d), lambda l, j: (0, 0)),
            pl.BlockSpec((1, d, d), lambda l, j: (l, 0, j)),
            pl.BlockSpec((1, 1, d), lambda l, j: (l, 0, j)),
        ],
        out_specs=pl.BlockSpec((1, rows, d), lambda l, j: (l, 0, j)),
        compiler_params=pltpu.CompilerParams(
            dimension_semantics=("arbitrary", "arbitrary"), vmem_limit_bytes=VMEM_LIMIT),
        name="mod",
    )(s_pad, mod_w, mod_b.reshape(depth, 1, n6))


def _glu_kernel(xm_ref, xp_ref, xn_ref, mod_ref, npre_ref, npost_ref, win_ref, cw_ref, wout_ref,
                o_ref, acc_ref, *, mode, tm, nf, fc, rows):
    sh_i, sc_i, gt_i = rows
    x_main = xm_ref[0]
    x_ext = jnp.concatenate([xp_ref[0], x_main, xn_ref[0]], axis=0)
    shift = mod_ref[0, sh_i:sh_i + 1, :]
    scale = mod_ref[0, sc_i:sc_i + 1, :]
    gate = mod_ref[0, gt_i:gt_i + 1, :]
    hb = (_rms(x_ext, npre_ref[...]) * (1.0 + scale) + shift).astype(BF16)
    valid = _valid_rows(tm)
    acc_ref[...] = jnp.zeros_like(acc_ref)

    def body(j, carry):
        p = _dot(hb, win_ref[j])
        if mode == "ffn":
            cin = p[:, :fc]
        else:
            cin = p[:, fc:2 * fc] * p[:, 2 * fc:]
        cin = jnp.where(valid, cin, 0.0)
        conv = _conv3(cin, cw_ref[j], tm)
        if mode == "ffn":
            g = _silu(conv) * p[HALO:HALO + tm, fc:]
        else:
            g = p[HALO:HALO + tm, :fc] * conv
        acc_ref[...] += _dot(g.astype(BF16), wout_ref[j])
        return carry

    lax.fori_loop(0, nf, body, 0)
    o_ref[0] = x_main + gate * _rms(acc_ref[...], npost_ref[...])


def _glu_call(x, mod, npre, npost, win, cw, wout, *, mode, rows, tm):
    bsz, t_len, d = x.shape
    nf, _, _ = win.shape
    fc = wout.shape[1]
    tm = min(tm, t_len)
    kern = functools.partial(_glu_kernel, mode=mode, tm=tm, nf=nf, fc=fc, rows=rows)
    return pl.pallas_call(
        kern,
        out_shape=jax.ShapeDtypeStruct(x.shape, F32),
        grid=(bsz, t_len // tm),
        in_specs=_halo_specs(tm, t_len, d) + [
            _mod_spec(mod),
            pl.BlockSpec((1, d), lambda b, t: (0, 0)),
            pl.BlockSpec((1, d), lambda b, t: (0, 0)),
            _resident(win.shape), _resident(cw.shape), _resident(wout.shape),
        ],
        out_specs=pl.BlockSpec((1, tm, d), lambda b, t: (b, t, 0)),
        scratch_shapes=[pltpu.VMEM((tm, d), F32)],
        compiler_params=pltpu.CompilerParams(
            dimension_semantics=("parallel", "arbitrary"), vmem_limit_bytes=VMEM_LIMIT),
        name="glu_" + mode,
    )(x, x, x, mod, npre, npost, win, cw, wout)


def _chunk_cols(w, branches, fc):
    d = w.shape[0]
    f = w.shape[1] // branches
    w = w.reshape(d, branches, f // fc, fc)
    return jnp.transpose(w, (2, 0, 1, 3)).reshape(f // fc, d, branches * fc)


def _chunk_conv(cw, fc):
    k, f = cw.shape
    return jnp.transpose(cw.reshape(k, f // fc, fc), (1, 0, 2))


def _proj_kernel(*refs, tm, chunk, q_side):
    (xm_ref, xp_ref, xn_ref, mod_ref, npre_ref, w_ref, cw_ref, alog_ref, dtb_ref,
     qn_ref, kn_ref, cos_ref, sin_ref) = refs[:13]
    outs = refs[13:]
    k_o, v_o, gc_o, gcT_o, akT_o, av_o = outs[:6]
    if q_side:
        q_o, z_o, aq_o = outs[6:9]

    x_main = xm_ref[0]
    x_ext = jnp.concatenate([xp_ref[0], x_main, xn_ref[0]], axis=0)
    shift = mod_ref[0, 0:1, :]
    scale = mod_ref[0, 1:2, :]
    npre = npre_ref[...]
    hb_ext = (_rms(x_ext, npre) * (1.0 + scale) + shift).astype(BF16)
    hb = (_rms(x_main, npre) * (1.0 + scale) + shift).astype(BF16)
    valid = _valid_rows(tm)

    groups = [(PC_K, k_o, True, 1.0), (PC_V, v_o, False, 1.0)]
    if q_side:
        groups.append((PC_Q, q_o, True, DN_DK ** -0.5))
    for c0, o_ref, unit, mult in groups:
        p = jnp.where(valid, _dot(hb_ext, w_ref[:, c0:c0 + DN_W]), 0.0)
        a = _silu(_conv3(p, cw_ref[:, c0:c0 + DN_W], tm))
        if unit:
            for hd in range(DN_HEADS):
                seg = a[:, hd * DN_DK:(hd + 1) * DN_DK]
                inv = lax.rsqrt(jnp.sum(seg * seg, axis=-1, keepdims=True) + EPS)
                o_ref[0, :, hd * DN_DK:(hd + 1) * DN_DK] = seg * (inv * mult)
        else:
            o_ref[0] = a

    lane = lax.broadcasted_iota(jnp.int32, (1, LANES), 1)
    half = (lane % ATT_HD) < (ATT_HD // 2)
    cos_t = cos_ref[...]
    sin_t = sin_ref[...]
    r_i = lax.broadcasted_iota(jnp.int32, (LANES, LANES), 0)
    c_i = lax.broadcasted_iota(jnp.int32, (LANES, LANES), 1)
    head_mean = jnp.where((r_i // ATT_HD) == (c_i // ATT_HD), 1.0 / ATT_HD, 0.0).astype(BF16)

    def head_norm_rope(xa, wn):
        hi, mid, lo = _split3(xa * xa)
        ms = _dot(hi, head_mean) + _dot(mid, head_mean) + _dot(lo, head_mean)
        xa = xa * lax.rsqrt(ms + EPS) * wn
        swapped = jnp.where(half, pltpu.roll(xa, LANES - ATT_HD // 2, 1), pltpu.roll(xa, ATT_HD // 2, 1))
        return xa * cos_t + swapped * sin_t

    if q_side:
        z_o[0] = _dot(hb, w_ref[:, PC_Z:PC_AQ])
        pq = _dot(hb, w_ref[:, PC_AQ:PC_AK])
        for gq in range(ATT_Q_W // LANES):
            xa = head_norm_rope(pq[:, gq * LANES:(gq + 1) * LANES], qn_ref[...])
            aq_o[0, :, gq * LANES:(gq + 1) * LANES] = (xa * (ATT_HD ** -0.5)).astype(BF16)

    pk = _dot(hb, w_ref[:, PC_AK:PC_END])
    ak = head_norm_rope(pk[:, :LANES], kn_ref[...])
    akT_o[0] = ak.T.astype(BF16)
    av_o[0] = pk[:, LANES:2 * LANES].astype(BF16)

    pab = pk[:, 2 * LANES:]
    n_dh = DN_DIRS * DN_HEADS
    zab = pab + dtb_ref[...]
    softplus = jnp.maximum(zab, 0.0) + jnp.log(1.0 + jnp.exp(-jnp.abs(zab)))
    g = -jnp.exp(alog_ref[...]) * softplus
    beta = 1.0 / (1.0 + jnp.exp(-pab))
    g = jnp.where(lane < n_dh, g, 0.0)
    rr = lax.broadcasted_iota(jnp.int32, (tm, tm), 0)
    cc = lax.broadcasted_iota(jnp.int32, (tm, tm), 1)
    same = (rr // chunk) == (cc // chunk)
    low = jnp.where(jnp.logical_and(same, cc <= rr), 1.0, 0.0).astype(BF16)
    upp = jnp.where(jnp.logical_and(same, cc >= rr), 1.0, 0.0).astype(BF16)
    hi, mid, lo = _split3(g)
    pre = _dot(low, hi) + _dot(low, mid) + _dot(low, lo)
    suf = _dot(upp, hi) + _dot(upp, mid) + _dot(upp, lo)
    gc = jnp.where(lane < DN_HEADS, pre, jnp.where(lane < n_dh, suf, jnp.where(lane < 2 * n_dh, beta, 0.0)))
    gc_o[0] = gc
    gcT_o[0] = gc.T[0:2 * n_dh, :]


def _proj_call(x, mod, npre, w, cw, alog, dtb, qn, kn, cos_t, sin_t, *, q_side, tm, chunk):
    bsz, t_len, d = x.shape
    tm = min(tm, t_len)
    nt = t_len // tm
    f32_out = lambda n: jax.ShapeDtypeStruct((bsz, t_len, n), F32)
    row_spec = lambda n: pl.BlockSpec((1, tm, n), lambda b, t: (b, t, 0))
    out_shape = [f32_out(DN_W), f32_out(DN_W), f32_out(LANES),
                 jax.ShapeDtypeStruct((bsz, 2 * DN_DIRS * DN_HEADS, t_len), F32),
                 jax.ShapeDtypeStruct((bsz, ATT_KV_W, t_len), BF16),
                 jax.ShapeDtypeStruct((bsz, t_len, ATT_KV_W), BF16)]
    out_specs = [row_spec(DN_W), row_spec(DN_W), row_spec(LANES),
                 pl.BlockSpec((1, 2 * DN_DIRS * DN_HEADS, tm), lambda b, t: (b, 0, t)),
                 pl.BlockSpec((1, ATT_KV_W, tm), lambda b, t: (b, 0, t)),
                 row_spec(ATT_KV_W)]
    if q_side:
        out_shape += [f32_out(DN_W), f32_out(DN_W), jax.ShapeDtypeStruct((bsz, t_len, ATT_Q_W), BF16)]
        out_specs += [row_spec(DN_W), row_spec(DN_W), row_spec(ATT_Q_W)]
    vec = lambda n: pl.BlockSpec((1, n), lambda b, t: (0, 0))
    kern = functools.partial(_proj_kernel, tm=tm, chunk=chunk, q_side=q_side)
    return pl.pallas_call(
        kern,
        out_shape=out_shape,
        grid=(bsz, nt),
        in_specs=_halo_specs(tm, t_len, d) + [
            _mod_spec(mod), vec(d), _resident(w.shape), _resident(cw.shape),
            vec(LANES), vec(LANES), vec(LANES), vec(LANES),
            pl.BlockSpec((tm, LANES), lambda b, t: (t, 0)),
            pl.BlockSpec((tm, LANES), lambda b, t: (t, 0)),
        ],
        out_specs=out_specs,
        compiler_params=pltpu.CompilerParams(
            dimension_semantics=("parallel", "arbitrary"), vmem_limit_bytes=VMEM_LIMIT),
        name="proj_q" if q_side else "proj_kv",
    )(x, x, x, mod, npre, w, cw, alog, dtb, qn, kn, cos_t, sin_t)


def _dn_kernel(*refs, chunk, sup, n_ctx, n_lat, ctx_q, hp):
    c = chunk
    nb = sup // c
    n_in_ctx = 5 if ctx_q else 4
    ctx_in = refs[:n_in_ctx]
    lat_in = refs[n_in_ctx:n_in_ctx + 5]
    n_out = 2 if ctx_q else 1
    outs = refs[n_in_ctx + 5:n_in_ctx + 5 + n_out]
    u_s, w_s, kd_s, qg_s, o_s, in_s, eg_s = refs[n_in_ctx + 5 + n_out:]
    head0 = pl.program_id(1) * hp
    chains = [(hh, d) for hh in range(hp) for d in range(DN_DIRS)]
    n_dh = DN_DIRS * DN_HEADS

    ri = lax.broadcasted_iota(jnp.int32, (sup, sup), 0)
    ci = lax.broadcasted_iota(jnp.int32, (sup, sup), 1)
    same = (ri // c) == (ci // c)
    incl = [jnp.logical_and(same, ci <= ri), jnp.logical_and(same, ci >= ri)]
    strict = [jnp.logical_and(same, ci < ri), jnp.logical_and(same, ci > ri)]
    rw = lax.broadcasted_iota(jnp.int32, (c, sup), 0)
    cw = lax.broadcasted_iota(jnp.int32, (c, sup), 1)
    cl = cw % c
    lane_blk = cw // c
    eye_w = jnp.where(rw == cl, 1.0, 0.0)
    level_mask = []
    n = 1
    while n < c:
        level_mask.append(jnp.logical_and(rw // (2 * n) == cl // (2 * n), rw // n != cl // n))
        n *= 2
    lane = lax.broadcasted_iota(jnp.int32, (1, LANES), 1)
    sub = lax.broadcasted_iota(jnp.int32, (2 * n_dh, 1), 0)

    def pick_col(x, idx):
        return jnp.sum(jnp.where(lane == idx, x, 0.0), axis=-1, keepdims=True)

    def block_diag(xw):
        return jnp.where(same, jnp.concatenate([xw] * nb, axis=0), 0.0).astype(BF16)

    def prep(src, j, dst, has_q):
        if has_q:
            k_ref, v_ref, gc_ref, gct_ref, q_ref = src
        else:
            k_ref, v_ref, gc_ref, gct_ref = src
        r0 = pl.multiple_of(j * sup, sup)
        d0 = pl.multiple_of(dst * sup, sup)
        gcb = gc_ref[0, pl.ds(r0, sup), :]
        gct = gct_ref[0, :, pl.ds(r0, sup)]
        st = []
        for hh, d in chains:
            hs = slice(hh * DN_DK, (hh + 1) * DN_DK)
            k = k_ref[0, pl.ds(r0, sup), hs]
            v = v_ref[0, pl.ds(r0, sup), hs]
            q = q_ref[0, pl.ds(r0, sup), hs] if has_q else None
            col = d * DN_HEADS + head0 + hh
            gc_col = pick_col(gcb, col)
            beta = pick_col(gcb, n_dh + col)
            gc_row = jnp.sum(jnp.where(sub == col, gct, 0.0), axis=0, keepdims=True)
            decay = jnp.where(incl[d], jnp.exp(jnp.where(incl[d], gc_col - gc_row, 0.0)), 0.0)
            kbeta = k * beta
            k16 = k.astype(BF16)
            if has_q:
                kk = _dot_nt(jnp.concatenate([kbeta, q], axis=0).astype(BF16), k16)
                low = jnp.where(strict[d], kk[:sup] * decay, 0.0)
                intra = kk[sup:] * decay
                for b in range(nb):
                    in_s[hh, d, pl.ds(pl.multiple_of(d0 + b * c, c), c), :] = (
                        intra[b * c:(b + 1) * c, b * c:(b + 1) * c].astype(BF16))
            else:
                low = jnp.where(strict[d], _dot_nt(kbeta.astype(BF16), k16) * decay, 0.0)
            low_w = jnp.where(lane_blk == 0, low[0:c], 0.0)
            for b in range(1, nb):
                low_w = low_w + jnp.where(lane_blk == b, low[b * c:(b + 1) * c], 0.0)
            st.append((k, v, q, gc_col, beta, kbeta, low_w))

        tw = [eye_w - jnp.where(level_mask[0], s[6], 0.0) for s in st]
        for lm in level_mask[1:]:
            et = [_dot(jnp.where(lm, s[6], 0.0).astype(BF16), block_diag(t)) for s, t in zip(st, tw)]
            tw = [t - _dot(t.astype(BF16), block_diag(e)) for t, e in zip(tw, et)]

        for (hh, d), s, t in zip(chains, st, tw):
            k, v, q, gc_col, beta, kbeta, _ = s
            egc = jnp.exp(gc_col)
            rhs = jnp.concatenate([v * beta, kbeta * egc], axis=1).astype(BF16)
            uw = _dot(block_diag(t), rhs)
            last = [b * c + (c - 1 if d == 0 else 0) for b in range(nb)]
            g_last = jnp.concatenate([jnp.broadcast_to(gc_col[i:i + 1], (c, 1)) for i in last], axis=0)
            u_s[hh, d, pl.ds(d0, sup), :] = uw[:, :DN_DK]
            w_s[hh, d, pl.ds(d0, sup), :] = uw[:, DN_DK:].astype(BF16)
            kd_s[hh, d, pl.ds(d0, sup), :] = (k * jnp.exp(g_last - gc_col)).astype(BF16)
            for b, i in enumerate(last):
                eg_s[hh, d, dst * nb + b] = jnp.broadcast_to(jnp.exp(gc_col[i:i + 1]), (HALO, LANES))
            if has_q:
                qg_s[hh, d, pl.ds(d0, sup), :] = (q * egc).astype(BF16)

    def prep_ctx(j, carry):
        prep(ctx_in, j, j, ctx_q)
        return carry

    def prep_lat(j, carry):
        prep(lat_in, j, n_ctx + j, True)
        return carry

    lax.fori_loop(0, n_ctx, prep_ctx, 0)
    lax.fori_loop(0, n_lat, prep_lat, 0)

    def scan_step(idx, states, has_q):
        rows = [pl.ds(pl.multiple_of(idx[d] * c, c), c) for _, d in chains]
        s16 = [s.astype(BF16) for s in states]
        if has_q:
            ws = [_dot(jnp.concatenate([w_s[hh, d, r, :], qg_s[hh, d, r, :]], axis=0), s_)
                  for (hh, d), r, s_ in zip(chains, rows, s16)]
            v16 = [(u_s[hh, d, r, :] - w[:c]).astype(BF16) for (hh, d), r, w in zip(chains, rows, ws)]
            for (hh, d), r, w, v_ in zip(chains, rows, ws, v16):
                o_s[hh, d, r, :] = w[c:] + _dot(in_s[hh, d, r, :], v_)
        else:
            v16 = [(u_s[hh, d, r, :] - _dot(w_s[hh, d, r, :], s_)).astype(BF16)
                   for (hh, d), r, s_ in zip(chains, rows, s16)]
        return tuple(s * eg_s[hh, d, idx[d], 0:1, :] + _dot_tn(kd_s[hh, d, r, :], v_)
                     for (hh, d), r, s, v_ in zip(chains, rows, states, v16))

    nc, nl = n_ctx * nb, n_lat * nb

    def scan_ctx(i, states):
        return scan_step((i, nc - 1 - i), states, ctx_q)

    def scan_lat(i, states):
        return scan_step((nc + i, nc + nl - 1 - i), states, True)

    zero = jnp.zeros((DN_DK, DN_DK), F32)
    states = lax.fori_loop(0, nc, scan_ctx, tuple(zero for _ in chains))
    lax.fori_loop(0, nl, scan_lat, states)

    t_ctx = nc * c
    for hh in range(hp):
        hs = slice(hh * DN_DK, (hh + 1) * DN_DK)
        outs[0][0, :, hs] = o_s[hh, 0, t_ctx:, :] + o_s[hh, 1, t_ctx:, :]
        if ctx_q:
            outs[1][0, :, hs] = o_s[hh, 0, :t_ctx, :] + o_s[hh, 1, :t_ctx, :]


def _dn_call(ctx_in, lat_in, *, chunk, sup, hp, ctx_q):
    bsz, t_lat, _ = lat_in[0].shape
    t_ctx = ctx_in[0].shape[1]
    assert t_ctx % sup == 0 and t_lat % sup == 0 and sup % chunk == 0 and DN_HEADS % hp == 0
    n_ctx, n_lat = t_ctx // sup, t_lat // sup
    t_all = t_ctx + t_lat
    n_rows = 2 * DN_DIRS * DN_HEADS

    def specs(t_len, has_q):
        head_blk = pl.BlockSpec((1, t_len, hp * DN_DK), lambda b, h: (b, 0, h))
        sp = [head_blk, head_blk,
              pl.BlockSpec((1, t_len, LANES), lambda b, h: (b, 0, 0)),
              pl.BlockSpec((1, n_rows, t_len), lambda b, h: (b, 0, 0))]
        if has_q:
            sp.append(head_blk)
        return sp

    out_shape = [jax.ShapeDtypeStruct((bsz, t_lat, DN_W), F32)]
    out_specs = [pl.BlockSpec((1, t_lat, hp * DN_DK), lambda b, h: (b, 0, h))]
    if ctx_q:
        out_shape.append(jax.ShapeDtypeStruct((bsz, t_ctx, DN_W), F32))
        out_specs.append(pl.BlockSpec((1, t_ctx, hp * DN_DK), lambda b, h: (b, 0, h)))
    nd = DN_DIRS
    scratch = [
        pltpu.VMEM((hp, nd, t_all, DN_DK), F32),
        pltpu.VMEM((hp, nd, t_all, DN_DK), BF16),
        pltpu.VMEM((hp, nd, t_all, DN_DK), BF16),
        pltpu.VMEM((hp, nd, t_all, DN_DK), BF16),
        pltpu.VMEM((hp, nd, t_all, DN_DK), F32),
        pltpu.VMEM((hp, nd, t_all, chunk), BF16),
        pltpu.VMEM((hp, nd, t_all // chunk, HALO, LANES), F32),
    ]
    kern = functools.partial(_dn_kernel, chunk=chunk, sup=sup, n_ctx=n_ctx, n_lat=n_lat, ctx_q=ctx_q, hp=hp)
    return pl.pallas_call(
        kern,
        out_shape=out_shape,
        grid=(bsz, DN_HEADS // hp),
        in_specs=specs(t_ctx, ctx_q) + specs(t_lat, True),
        out_specs=out_specs,
        scratch_shapes=scratch,
        compiler_params=pltpu.CompilerParams(
            dimension_semantics=("parallel", "arbitrary"), vmem_limit_bytes=VMEM_LIMIT),
        name="dn_q" if ctx_q else "dn_kv",
    )(*ctx_in, *lat_in)


def _attn_kernel(*refs, n_sets):
    q_ref = refs[0]
    kv = refs[1:1 + 2 * n_sets]
    o_ref = refs[1 + 2 * n_sets]
    for g in range(ATT_KV_HEADS):
        for hh in range(ATT_GROUPS):
            h0 = (g * ATT_GROUPS + hh) * ATT_HD
            q = q_ref[0, :, h0:h0 + ATT_HD]
            scores = [_dot(q, kv[2 * i][0, g * ATT_HD:(g + 1) * ATT_HD, :]) for i in range(n_sets)]
            m = scores[0].max(axis=-1, keepdims=True)
            for s in scores[1:]:
                m = jnp.maximum(m, s.max(axis=-1, keepdims=True))
            den = 0.0
            acc = 0.0
            for i, s in enumerate(scores):
                p = jnp.exp(s - m)
                den = den + jnp.sum(p, axis=-1, keepdims=True)
                acc = acc + _dot(p.astype(BF16), kv[2 * i + 1][0])
            o_ref[0, :, h0:h0 + ATT_HD] = (acc[:, g * ATT_HD:(g + 1) * ATT_HD] / den).astype(BF16)


def _attn_call(q, kv_sets, *, tq):
    bsz, t_len, _ = q.shape
    tq = min(tq, t_len)
    in_specs = [pl.BlockSpec((1, tq, ATT_Q_W), lambda b, t: (b, t, 0))]
    args = [q]
    for k_t, v in kv_sets:
        s_len = v.shape[1]
        in_specs += [pl.BlockSpec((1, ATT_KV_W, s_len), lambda b, t: (b, 0, 0)),
                     pl.BlockSpec((1, s_len, ATT_KV_W), lambda b, t: (b, 0, 0))]
        args += [k_t, v]
    return pl.pallas_call(
        functools.partial(_attn_kernel, n_sets=len(kv_sets)),
        out_shape=jax.ShapeDtypeStruct((bsz, t_len, ATT_Q_W), BF16),
        grid=(bsz, t_len // tq),
        in_specs=in_specs,
        out_specs=pl.BlockSpec((1, tq, ATT_Q_W), lambda b, t: (b, t, 0)),
        compiler_params=pltpu.CompilerParams(
            dimension_semantics=("parallel", "arbitrary"), vmem_limit_bytes=VMEM_LIMIT),
        name="attn%d" % len(kv_sets),
    )(*args)


def _mixout_kernel(x_ref, o_ref, z_ref, b_ref, mod_ref, dnn_ref, npost_ref, w_ref, out_ref):
    gate = mod_ref[0, 2:3, :]
    o = o_ref[0]
    z = z_ref[0]
    parts = []
    for hd in range(DN_HEADS):
        seg = o[:, hd * DN_DK:(hd + 1) * DN_DK]
        parts.append(_rms(seg, dnn_ref[...]) * _silu(z[:, hd * DN_DK:(hd + 1) * DN_DK]))
    a = jnp.concatenate(parts, axis=1).astype(BF16)
    y = _dot(a, w_ref[0:DN_W, :]) + _dot(b_ref[0], w_ref[DN_W:, :])
    out_ref[0] = x_ref[0] + gate * _rms(y, npost_ref[...])


def _mixout_call(x, o_dn, z, b_att, mod, dnn, npost, w_out, *, tm):
    bsz, t_len, d = x.shape
    tm = min(tm, t_len)
    row = lambda n: pl.BlockSpec((1, tm, n), lambda b, t: (b, t, 0))
    return pl.pallas_call(
        _mixout_kernel,
        out_shape=jax.ShapeDtypeStruct(x.shape, F32),
        grid=(bsz, t_len // tm),
        in_specs=[row(d), row(DN_W), row(DN_W), row(ATT_Q_W), _mod_spec(mod),
                  pl.BlockSpec((1, DN_DK), lambda b, t: (0, 0)),
                  pl.BlockSpec((1, d), lambda b, t: (0, 0)),
                  _resident(w_out.shape)],
        out_specs=row(d),
        compiler_params=pltpu.CompilerParams(
            dimension_semantics=("parallel", "arbitrary"), vmem_limit_bytes=VMEM_LIMIT),
        name="mixout",
    )(x, o_dn, z, b_att, mod, dnn, npost, w_out)


def _rope_tables(n):
    half = ATT_HD // 2
    r = (jnp.arange(n) // GRID_W).astype(F32)
    col = (jnp.arange(n) % GRID_W).astype(F32)
    inv = ROPE_THETA ** (-jnp.arange(0, half, 2, dtype=F32) / half)
    ang = jnp.concatenate([r[:, None] * inv, col[:, None] * inv], axis=-1)
    cos, sin = jnp.cos(ang), jnp.sin(ang)
    cos_t = jnp.tile(jnp.concatenate([cos, cos], axis=-1), (1, LANES // ATT_HD))
    sin_t = jnp.tile(jnp.concatenate([-sin, sin], axis=-1), (1, LANES // ATT_HD))
    return cos_t, sin_t


def _even_weights(w_in, q_norm, k_norm, a_log, dt_bias):
    kv_cols = 2 * DN_W + 2 * DN_DIRS * DN_HEADS + 2 * ATT_KV_W
    perm = np.concatenate([np.arange(0, ATT_HD, 2), np.arange(1, ATT_HD, 2)])
    off_ab = 2 * DN_W
    off_ak = off_ab + 2 * DN_DIRS * DN_HEADS
    off_av = off_ak + ATT_KV_W
    off_q = kv_cols
    off_z = off_q + DN_W
    off_aq = off_z + DN_W
    aq_idx = np.concatenate([off_aq + h * ATT_HD + perm for h in range(ATT_HEADS)])
    ak_idx = np.concatenate([off_ak + h * ATT_HD + perm for h in range(ATT_KV_HEADS)])
    cols = np.concatenate([
        np.arange(0, 2 * DN_W), np.arange(off_q, off_q + DN_W), np.arange(off_z, off_z + DN_W),
        aq_idx, ak_idx, np.arange(off_av, off_av + ATT_KV_W), np.arange(off_ab, off_ab + 2 * DN_DIRS * DN_HEADS)])
    w = jnp.take(w_in, jnp.asarray(cols), axis=1)
    w = jnp.pad(w, ((0, 0), (0, PC_END - w.shape[1]))).astype(BF16)
    tile2 = lambda v: jnp.tile(v[perm], LANES // ATT_HD).reshape(1, LANES)
    pad = lambda v: jnp.pad(v.reshape(-1), (0, LANES - v.size)).reshape(1, LANES)
    return w, tile2(q_norm), tile2(k_norm), pad(a_log), pad(dt_bias)


def kernel(x, c, ctx, c_ctx, mod_w, mod_b, norm_mix_pre, norm_mix_post, norm_ffn_pre, norm_ffn_post,
           hyb_w_in, hyb_w_out, dn_conv_w, dn_a_log, dn_dt_bias, dn_out_norm, att_q_norm, att_k_norm,
           sc_w_in, sc_conv_w, sc_w_out, ffn_w_up, ffn_conv_w, ffn_w_down):
    bsz, seq, d = x.shape
    depth = mod_w.shape[0]
    ctx_len = ctx.shape[1]
    fc = 256
    tm_glu, tm_proj, tm_out, tq = 512, 256, 512, 256

    rows = ((bsz + 1 + HALO - 1) // HALO) * HALO
    s_pad = jnp.zeros((rows, d), F32).at[:bsz].set(c).at[bsz].set(c_ctx)
    mod_all = _mod_call(s_pad, mod_w, mod_b).reshape(depth, rows, 6, d)

    cos_l, sin_l = _rope_tables(seq)
    cos_c, sin_c = jnp.ones((ctx_len, LANES), F32), jnp.zeros((ctx_len, LANES), F32)
    vec = lambda v: v.reshape(1, -1)

    for layer in range(depth):
        even = layer % 2 == 0
        ctx_live = any(j % 2 == 0 for j in range(layer + 1, depth))
        mod_l = mod_all[layer, :bsz]
        mod_c = mod_all[layer, bsz:bsz + 1]
        npre, npost = vec(norm_mix_pre[layer]), vec(norm_mix_post[layer])
        if even:
            e = layer // 2
            w, qn, kn, alog, dtb = _even_weights(hyb_w_in[e], att_q_norm[e], att_k_norm[e],
                                                 dn_a_log[e], dn_dt_bias[e])
            cw = dn_conv_w[e]
            w_out = hyb_w_out[e].astype(BF16)
            dnn = vec(dn_out_norm[e])
            proj = functools.partial(_proj_call, npre=npre, w=w, cw=cw, alog=alog, dtb=dtb, qn=qn, kn=kn,
                                     tm=tm_proj, chunk=DN_CHUNK)
            pl_ = proj(x, mod_l, cos_t=cos_l, sin_t=sin_l, q_side=True)
            pc_ = proj(ctx, mod_c, cos_t=cos_c, sin_t=sin_c, q_side=ctx_live)
            k_l, v_l, gc_l, gct_l, akt_l, av_l, q_l, z_l, aq_l = pl_
            k_c, v_c, gc_c, gct_c, akt_c, av_c = pc_[:6]
            lat_in = (k_l, v_l, gc_l, gct_l, q_l)
            ctx_in = (k_c, v_c, gc_c, gct_c) + ((pc_[6],) if ctx_live else ())
            o_dn = _dn_call(ctx_in, lat_in, chunk=DN_CHUNK, sup=DN_SUPER, hp=DN_HEADS_PER_STEP, ctx_q=ctx_live)
            b_lat = _attn_call(aq_l, [(akt_c, av_c), (akt_l, av_l)], tq=tq)
            x_new = _mixout_call(x, o_dn[0], z_l, b_lat, mod_l, dnn, npost, w_out, tm=tm_out)
            if ctx_live:
                b_ctx = _attn_call(pc_[8], [(akt_c, av_c)], tq=tq)
                ctx = _mixout_call(ctx, o_dn[1], pc_[7], b_ctx, mod_c, dnn, npost, w_out, tm=tm_out)
            x = x_new
        else:
            o = layer // 2
            win = _chunk_cols(sc_w_in[o], 3, fc).astype(BF16)
            cwc = _chunk_conv(sc_conv_w[o], fc)
            wout = sc_w_out[o].reshape(-1, fc, d).astype(BF16)
            x = _glu_call(x, mod_l, npre, npost, win, cwc, wout, mode="sc", rows=(0, 1, 2), tm=tm_glu)
            if ctx_live:
                ctx = _glu_call(ctx, mod_c, npre, npost, win, cwc, wout, mode="sc", rows=(0, 1, 2), tm=tm_glu)
        win = _chunk_cols(ffn_w_up[layer], 2, fc).astype(BF16)
        cwc = _chunk_conv(ffn_conv_w[layer], fc)
        wout = ffn_w_down[layer].reshape(-1, fc, d).astype(BF16)
        npre, npost = vec(norm_ffn_pre[layer]), vec(norm_ffn_post[layer])
        x = _glu_call(x, mod_l, npre, npost, win, cwc, wout, mode="ffn", rows=(3, 4, 5), tm=tm_glu)
        if ctx_live:
            ctx = _glu_call(ctx, mod_c, npre, npost, win, cwc, wout, mode="ffn", rows=(3, 4, 5), tm=tm_glu)
    return x
```

```python
import functools

import jax
import jax.numpy as jnp
import numpy as np
from jax import lax
from jax.experimental import pallas as pl
from jax.experimental.pallas import tpu as pltpu

F32 = jnp.float32
BF16 = jnp.bfloat16
EPS = 1e-6
HALO = 8
LANES = 128

D_MODEL = 1024
GRID_W = 64
ROPE_THETA = 10000.0
DN_HEADS = 4
DN_DK = 128
DN_DIRS = 2
DN_CHUNK = 64
DN_SUPER = 256
DN_HEADS_PER_STEP = 4
ATT_HEADS = 8
ATT_KV_HEADS = 2
ATT_HD = 64
ATT_GROUPS = ATT_HEADS // ATT_KV_HEADS
D_FF = 2816
DN_W = DN_HEADS * DN_DK
ATT_Q_W = ATT_HEADS * ATT_HD
ATT_KV_W = ATT_KV_HEADS * ATT_HD

PC_K, PC_V, PC_Q, PC_Z, PC_AQ, PC_AK, PC_AV, PC_AB, PC_END = (
    0, 512, 1024, 1536, 2048, 2560, 2688, 2816, 2944)

VMEM_LIMIT = 56 * 1024 * 1024


def _resident(shape):
    nd = len(shape)
    return pl.BlockSpec(shape, lambda *_: (0,) * nd, pipeline_mode=pl.Buffered(1))


def _rms(x, w):
    return x * lax.rsqrt(jnp.mean(x * x, axis=-1, keepdims=True) + EPS) * w


def _silu(x):
    return x * (1.0 / (1.0 + jnp.exp(-x)))


def _dot(a, b):
    return jnp.dot(a, b, preferred_element_type=F32)


def _dot_nt(a, b):
    return lax.dot_general(a, b, (((1,), (1,)), ((), ())), preferred_element_type=F32)


def _dot_tn(a, b):
    return lax.dot_general(a, b, (((0,), (0,)), ((), ())), preferred_element_type=F32)


def _split3(x):
    hi = x.astype(BF16)
    r = x - hi.astype(F32)
    mid = r.astype(BF16)
    lo = (r - mid.astype(F32)).astype(BF16)
    return hi, mid, lo


def _valid_rows(tm):
    t = pl.program_id(1)
    nt = pl.num_programs(1)
    r = lax.broadcasted_iota(jnp.int32, (tm + 2 * HALO, 1), 0)
    lo_ok = jnp.logical_or(r >= HALO, t > 0)
    hi_ok = jnp.logical_or(r < tm + HALO, t < nt - 1)
    return jnp.logical_and(lo_ok, hi_ok)


def _conv3(cin, cw, tm):
    return (cw[0:1] * cin[HALO - 1:HALO - 1 + tm] + cw[1:2] * cin[HALO:HALO + tm]
            + cw[2:3] * cin[HALO + 1:HALO + 1 + tm])


def _halo_specs(tm, t_len, d):
    tb = tm // HALO
    nhb = t_len // HALO
    return [
        pl.BlockSpec((1, tm, d), lambda b, t: (b, t, 0)),
        pl.BlockSpec((1, HALO, d), lambda b, t: (b, jnp.maximum(t * tb - 1, 0), 0)),
        pl.BlockSpec((1, HALO, d), lambda b, t: (b, jnp.minimum((t + 1) * tb, nhb - 1), 0)),
    ]


def _mod_spec(mod):
    if mod.shape[0] > 1:
        return pl.BlockSpec((1, 6, mod.shape[2]), lambda b, t: (b, 0, 0))
    return pl.BlockSpec((1, 6, mod.shape[2]), lambda b, t: (0, 0, 0))


def _mod_kernel(s_ref, w_ref, b_ref, o_ref):
    s = _silu(s_ref[...])
    o_ref[0] = jnp.dot(s, w_ref[0], preferred_element_type=F32,
                       precision=lax.Precision.HIGHEST) + b_ref[0]


def _mod_call(s_pad, mod_w, mod_b):
    depth, d, n6 = mod_w.shape
    rows = s_pad.shape[0]
    nb = n6 // d
    return pl.pallas_call(
        _mod_kernel,
        out_shape=jax.ShapeDtypeStruct((depth, rows, n6), F32),
        grid=(depth, nb),
        in_specs=[
            pl.BlockSpec((rows, d), lambda l, j: (0, 0)),
            pl.BlockSpec((1, d, d), lambda l, j: (l, 0, j)),
            pl.BlockSpec((1, 1, d), lambda l, j: (l, 0, j)),
        ],
        out_specs=pl.BlockSpec((1, rows, d), lambda l, j: (l, 0, j)),
        compiler_params=pltpu.CompilerParams(
            dimension_semantics=("arbitrary", "arbitrary"), vmem_limit_bytes=VMEM_LIMIT),
        name="mod",
    )(s_pad, mod_w, mod_b.reshape(depth, 1, n6))


def _glu_kernel(xm_ref, xp_ref, xn_ref, mod_ref, npre_ref, npost_ref, win_ref, cw_ref, wout_ref,
                o_ref, acc_ref, *, mode, tm, nf, fc, rows):
    sh_i, sc_i, gt_i = rows
    x_main = xm_ref[0]
    x_ext = jnp.concatenate([xp_ref[0], x_main, xn_ref[0]], axis=0)
    shift = mod_ref[0, sh_i:sh_i + 1, :]
    scale = mod_ref[0, sc_i:sc_i + 1, :]
    gate = mod_ref[0, gt_i:gt_i + 1, :]
    hb = (_rms(x_ext, npre_ref[...]) * (1.0 + scale) + shift).astype(BF16)
    valid = _valid_rows(tm)
    def finish(j, p):
        if mode == "ffn":
            cin = p[:, :fc]
        else:
            cin = p[:, fc:2 * fc] * p[:, 2 * fc:]
        conv = _conv3(jnp.where(valid, cin, 0.0), cw_ref[j], tm)
        if mode == "ffn":
            g = _silu(conv) * p[HALO:HALO + tm, fc:]
        else:
            g = p[HALO:HALO + tm, :fc] * conv
        y = _dot(g.astype(BF16), wout_ref[j])
        if j == 0:
            acc_ref[...] = y
        else:
            acc_ref[...] += y

    p = _dot(hb, win_ref[0])
    for j in range(nf):
        p_next = _dot(hb, win_ref[j + 1]) if j + 1 < nf else None
        finish(j, p)
        p = p_next
    o_ref[0] = x_main + gate * _rms(acc_ref[...], npost_ref[...])


def _glu_call(x, mod, npre, npost, win, cw, wout, *, mode, rows, tm):
    bsz, t_len, d = x.shape
    nf, _, _ = win.shape
    fc = wout.shape[1]
    tm = min(tm, t_len)
    kern = functools.partial(_glu_kernel, mode=mode, tm=tm, nf=nf, fc=fc, rows=rows)
    return pl.pallas_call(
        kern,
        out_shape=jax.ShapeDtypeStruct(x.shape, F32),
        grid=(bsz, t_len // tm),
        in_specs=_halo_specs(tm, t_len, d) + [
            _mod_spec(mod),
            pl.BlockSpec((1, d), lambda b, t: (0, 0)),
            pl.BlockSpec((1, d), lambda b, t: (0, 0)),
            _resident(win.shape), _resident(cw.shape), _resident(wout.shape),
        ],
        out_specs=pl.BlockSpec((1, tm, d), lambda b, t: (b, t, 0)),
        scratch_shapes=[pltpu.VMEM((tm, d), F32)],
        compiler_params=pltpu.CompilerParams(
            dimension_semantics=("parallel", "arbitrary"), vmem_limit_bytes=VMEM_LIMIT),
        name="glu_" + mode,
    )(x, x, x, mod, npre, npost, win, cw, wout)


def _chunk_cols(w, branches, fc):
    d = w.shape[0]
    f = w.shape[1] // branches
    w = w.reshape(d, branches, f // fc, fc)
    return jnp.transpose(w, (2, 0, 1, 3)).reshape(f // fc, d, branches * fc)


def _chunk_conv(cw, fc):
    k, f = cw.shape
    return jnp.transpose(cw.reshape(k, f // fc, fc), (1, 0, 2))


def _proj_kernel(*refs, tm, chunk, q_side):
    (xm_ref, xp_ref, xn_ref, mod_ref, npre_ref, w_ref, cw_ref, alog_ref, dtb_ref,
     qn_ref, kn_ref, cos_ref, sin_ref) = refs[:13]
    outs = refs[13:]
    k_o, v_o, gc_o, gcT_o, akT_o, av_o = outs[:6]
    if q_side:
        q_o, z_o, aq_o = outs[6:9]

    x_main = xm_ref[0]
    x_ext = jnp.concatenate([xp_ref[0], x_main, xn_ref[0]], axis=0)
    shift = mod_ref[0, 0:1, :]
    scale = mod_ref[0, 1:2, :]
    npre = npre_ref[...]
    hb_ext = (_rms(x_ext, npre) * (1.0 + scale) + shift).astype(BF16)
    hb = (_rms(x_main, npre) * (1.0 + scale) + shift).astype(BF16)
    valid = _valid_rows(tm)

    groups = [(PC_K, k_o, True, 1.0), (PC_V, v_o, False, 1.0)]
    if q_side:
        groups.append((PC_Q, q_o, True, DN_DK ** -0.5))
    for c0, o_ref, unit, mult in groups:
        p = jnp.where(valid, _dot(hb_ext, w_ref[:, c0:c0 + DN_W]), 0.0)
        a = _silu(_conv3(p, cw_ref[:, c0:c0 + DN_W], tm))
        if unit:
            for hd in range(DN_HEADS):
                seg = a[:, hd * DN_DK:(hd + 1) * DN_DK]
                inv = lax.rsqrt(jnp.sum(seg * seg, axis=-1, keepdims=True) + EPS)
                o_ref[0, :, hd * DN_DK:(hd + 1) * DN_DK] = (seg * (inv * mult)).astype(BF16)
        else:
            o_ref[0] = a.astype(BF16)

    lane = lax.broadcasted_iota(jnp.int32, (1, LANES), 1)
    half = (lane % ATT_HD) < (ATT_HD // 2)
    cos_t = cos_ref[...]
    sin_t = sin_ref[...]
    r_i = lax.broadcasted_iota(jnp.int32, (LANES, LANES), 0)
    c_i = lax.broadcasted_iota(jnp.int32, (LANES, LANES), 1)
    head_mean = jnp.where((r_i // ATT_HD) == (c_i // ATT_HD), 1.0 / ATT_HD, 0.0).astype(BF16)

    def head_norm_rope(xa, wn):
        hi, mid, lo = _split3(xa * xa)
        ms = _dot(hi, head_mean) + _dot(mid, head_mean) + _dot(lo, head_mean)
        xa = xa * lax.rsqrt(ms + EPS) * wn
        swapped = jnp.where(half, pltpu.roll(xa, LANES - ATT_HD // 2, 1), pltpu.roll(xa, ATT_HD // 2, 1))
        return xa * cos_t + swapped * sin_t

    if q_side:
        z_o[0] = _dot(hb, w_ref[:, PC_Z:PC_AQ])
        pq = _dot(hb, w_ref[:, PC_AQ:PC_AK])
        for gq in range(ATT_Q_W // LANES):
            xa = head_norm_rope(pq[:, gq * LANES:(gq + 1) * LANES], qn_ref[...])
            aq_o[0, :, gq * LANES:(gq + 1) * LANES] = (xa * (ATT_HD ** -0.5)).astype(BF16)

    pk = _dot(hb, w_ref[:, PC_AK:PC_END])
    ak = head_norm_rope(pk[:, :LANES], kn_ref[...])
    akT_o[0] = ak.T.astype(BF16)
    av_o[0] = pk[:, LANES:2 * LANES].astype(BF16)

    pab = pk[:, 2 * LANES:]
    n_dh = DN_DIRS * DN_HEADS
    zab = pab + dtb_ref[...]
    softplus = jnp.maximum(zab, 0.0) + jnp.log(1.0 + jnp.exp(-jnp.abs(zab)))
    g = -jnp.exp(alog_ref[...]) * softplus
    beta = 1.0 / (1.0 + jnp.exp(-pab))
    g = jnp.where(lane < n_dh, g, 0.0)
    rr = lax.broadcasted_iota(jnp.int32, (tm, tm), 0)
    cc = lax.broadcasted_iota(jnp.int32, (tm, tm), 1)
    same = (rr // chunk) == (cc // chunk)
    low = jnp.where(jnp.logical_and(same, cc <= rr), 1.0, 0.0).astype(BF16)
    upp = jnp.where(jnp.logical_and(same, cc >= rr), 1.0, 0.0).astype(BF16)
    hi, mid, lo = _split3(g)
    pre = _dot(low, hi) + _dot(low, mid) + _dot(low, lo)
    suf = _dot(upp, hi) + _dot(upp, mid) + _dot(upp, lo)
    gc = jnp.where(lane < DN_HEADS, pre, jnp.where(lane < n_dh, suf, jnp.where(lane < 2 * n_dh, beta, 0.0)))
    gc_o[0] = gc
    gcT_o[0] = gc.T[0:2 * n_dh, :]


def _proj_call(x, mod, npre, w, cw, alog, dtb, qn, kn, cos_t, sin_t, *, q_side, tm, chunk):
    bsz, t_len, d = x.shape
    tm = min(tm, t_len)
    nt = t_len // tm
    f32_out = lambda n: jax.ShapeDtypeStruct((bsz, t_len, n), F32)
    bf16_out = lambda n: jax.ShapeDtypeStruct((bsz, t_len, n), BF16)
    row_spec = lambda n: pl.BlockSpec((1, tm, n), lambda b, t: (b, t, 0))
    out_shape = [bf16_out(DN_W), bf16_out(DN_W), f32_out(LANES),
                 jax.ShapeDtypeStruct((bsz, 2 * DN_DIRS * DN_HEADS, t_len), F32),
                 jax.ShapeDtypeStruct((bsz, ATT_KV_W, t_len), BF16),
                 jax.ShapeDtypeStruct((bsz, t_len, ATT_KV_W), BF16)]
    out_specs = [row_spec(DN_W), row_spec(DN_W), row_spec(LANES),
                 pl.BlockSpec((1, 2 * DN_DIRS * DN_HEADS, tm), lambda b, t: (b, 0, t)),
                 pl.BlockSpec((1, ATT_KV_W, tm), lambda b, t: (b, 0, t)),
                 row_spec(ATT_KV_W)]
    if q_side:
        out_shape += [bf16_out(DN_W), f32_out(DN_W), bf16_out(ATT_Q_W)]
        out_specs += [row_spec(DN_W), row_spec(DN_W), row_spec(ATT_Q_W)]
    vec = lambda n: pl.BlockSpec((1, n), lambda b, t: (0, 0))
    kern = functools.partial(_proj_kernel, tm=tm, chunk=chunk, q_side=q_side)
    return pl.pallas_call(
        kern,
        out_shape=out_shape,
        grid=(bsz, nt),
        in_specs=_halo_specs(tm, t_len, d) + [
            _mod_spec(mod), vec(d), _resident(w.shape), _resident(cw.shape),
            vec(LANES), vec(LANES), vec(LANES), vec(LANES),
            pl.BlockSpec((tm, LANES), lambda b, t: (t, 0)),
            pl.BlockSpec((tm, LANES), lambda b, t: (t, 0)),
        ],
        out_specs=out_specs,
        compiler_params=pltpu.CompilerParams(
            dimension_semantics=("parallel", "arbitrary"), vmem_limit_bytes=VMEM_LIMIT),
        name="proj_q" if q_side else "proj_kv",
    )(x, x, x, mod, npre, w, cw, alog, dtb, qn, kn, cos_t, sin_t)


def _dn_kernel(*refs, chunk, sup, n_ctx, n_lat, ctx_q, hp):
    c = chunk
    nb = sup // c
    n_in_ctx = 5 if ctx_q else 4
    ctx_in = refs[:n_in_ctx]
    lat_in = refs[n_in_ctx:n_in_ctx + 5]
    n_out = 2 if ctx_q else 1
    outs = refs[n_in_ctx + 5:n_in_ctx + 5 + n_out]
    u_s, w_s, kd_s, qg_s, in_s, eg_s = refs[n_in_ctx + 5 + n_out:]
    head0 = pl.program_id(1) * hp
    chains = [(hh, d) for hh in range(hp) for d in range(DN_DIRS)]
    n_dh = DN_DIRS * DN_HEADS

    ri = lax.broadcasted_iota(jnp.int32, (sup, sup), 0)
    ci = lax.broadcasted_iota(jnp.int32, (sup, sup), 1)
    same = (ri // c) == (ci // c)
    incl = [jnp.logical_and(same, ci <= ri), jnp.logical_and(same, ci >= ri)]
    strict = [jnp.logical_and(same, ci < ri), jnp.logical_and(same, ci > ri)]
    rw = lax.broadcasted_iota(jnp.int32, (c, sup), 0)
    cw = lax.broadcasted_iota(jnp.int32, (c, sup), 1)
    cl = cw % c
    lane_blk = cw // c
    eye_w = jnp.where(rw == cl, 1.0, 0.0)
    level_mask = []
    n = 1
    while n < c:
        level_mask.append(jnp.logical_and(rw // (2 * n) == cl // (2 * n), rw // n != cl // n))
        n *= 2
    lane = lax.broadcasted_iota(jnp.int32, (1, LANES), 1)
    sub = lax.broadcasted_iota(jnp.int32, (2 * n_dh, 1), 0)

    def pick_col(x, idx):
        return jnp.sum(jnp.where(lane == idx, x, 0.0), axis=-1, keepdims=True)

    def block_diag(xw):
        return jnp.where(same, jnp.concatenate([xw] * nb, axis=0), 0.0).astype(BF16)

    def prep(slot, src, pos, has_q):
        if has_q:
            k_ref, v_ref, gc_ref, gct_ref, q_ref = src
        else:
            k_ref, v_ref, gc_ref, gct_ref = src
        r0 = [pl.multiple_of(pos[d] * sup, sup) for d in range(DN_DIRS)]
        gcb = [gc_ref[0, pl.ds(r0[d], sup), :] for d in range(DN_DIRS)]
        gct = [gct_ref[0, :, pl.ds(r0[d], sup)] for d in range(DN_DIRS)]
        st = []
        for n, (hh, d) in enumerate(chains):
            hs = slice(hh * DN_DK, (hh + 1) * DN_DK)
            k16 = k_ref[0, pl.ds(r0[d], sup), hs]
            k = k16.astype(F32)
            v = v_ref[0, pl.ds(r0[d], sup), hs].astype(F32)
            q = q_ref[0, pl.ds(r0[d], sup), hs].astype(F32) if has_q else None
            col = d * DN_HEADS + head0 + hh
            gc_col = pick_col(gcb[d], col)
            beta = pick_col(gcb[d], n_dh + col)
            gc_row = jnp.sum(jnp.where(sub == col, gct[d], 0.0), axis=0, keepdims=True)
            decay = jnp.where(incl[d], jnp.exp(jnp.where(incl[d], gc_col - gc_row, 0.0)), 0.0)
            kbeta = k * beta
            if has_q:
                kk = _dot_nt(jnp.concatenate([kbeta, q], axis=0).astype(BF16), k16)
                low = jnp.where(strict[d], kk[:sup] * decay, 0.0)
                intra = kk[sup:] * decay
                for b in range(nb):
                    in_s[slot, n, b * c:(b + 1) * c, :] = intra[b * c:(b + 1) * c, b * c:(b + 1) * c].astype(BF16)
            else:
                low = jnp.where(strict[d], _dot_nt(kbeta.astype(BF16), k16) * decay, 0.0)
            low_w = jnp.where(lane_blk == 0, low[0:c], 0.0)
            for b in range(1, nb):
                low_w = low_w + jnp.where(lane_blk == b, low[b * c:(b + 1) * c], 0.0)
            egc = jnp.exp(gc_col)
            rhs = jnp.concatenate([v * beta, kbeta * egc], axis=1).astype(BF16)
            last = [b * c + (c - 1 if d == 0 else 0) for b in range(nb)]
            g_last = jnp.concatenate([jnp.broadcast_to(gc_col[i:i + 1], (c, 1)) for i in last], axis=0)
            kd_s[slot, n] = (k * jnp.exp(g_last - gc_col)).astype(BF16)
            for b, i in enumerate(last):
                eg_s[slot, n, b] = jnp.broadcast_to(jnp.exp(gc_col[i:i + 1]), (HALO, LANES))
            if has_q:
                qg_s[slot, n] = (q * egc).astype(BF16)
            st.append((low_w, rhs))
            yield

        tw = [eye_w - jnp.where(level_mask[0], s[0], 0.0) for s in st]
        for lm in level_mask[1:]:
            et = [_dot(jnp.where(lm, s[0], 0.0).astype(BF16), block_diag(t)) for s, t in zip(st, tw)]
            yield
            tw = [t - _dot(t.astype(BF16), block_diag(e)) for t, e in zip(tw, et)]
            yield

        for n, (s, t) in enumerate(zip(st, tw)):
            uw = _dot(block_diag(t), s[1])
            u_s[slot, n] = uw[:, :DN_DK]
            w_s[slot, n] = uw[:, DN_DK:].astype(BF16)
            yield

    def scan_step(slot, t, states, rows_out, o_ref, has_q):
        new = []
        rows = [slice(t * c, (t + 1) * c), slice((nb - 1 - t) * c, (nb - t) * c)]
        blk = [t, nb - 1 - t]
        s16 = [s.astype(BF16) for s in states]
        if has_q:
            ws = [_dot(jnp.concatenate([w_s[slot, n, rows[d], :], qg_s[slot, n, rows[d], :]], axis=0), s16[n])
                  for n, (hh, d) in enumerate(chains)]
            v16 = [(u_s[slot, n, rows[d], :] - ws[n][:c]).astype(BF16) for n, (hh, d) in enumerate(chains)]
            for n, (hh, d) in enumerate(chains):
                o = ws[n][c:] + _dot(in_s[slot, n, rows[d], :], v16[n])
                dst = pl.ds(pl.multiple_of(rows_out[d] + blk[d] * c, c), c)
                o_ref[0, dst, hh * DN_DK:(hh + 1) * DN_DK] += o
        else:
            v16 = [(u_s[slot, n, rows[d], :] - _dot(w_s[slot, n, rows[d], :], s16[n])).astype(BF16)
                   for n, (hh, d) in enumerate(chains)]
        for n, (hh, d) in enumerate(chains):
            new.append(states[n] * eg_s[slot, n, blk[d], 0:1, :] + _dot_tn(kd_s[slot, n, rows[d], :], v16[n]))
        return tuple(new)

    def run(prep_gen, slot, states, rows_out, o_ref, has_q):
        holder = [states]
        todo = list(range(nb)) if slot is not None else []
        if prep_gen is not None:
            n_stage = 2 * len(chains) + 2 * (len(level_mask) - 1)
            every = max(n_stage // (nb + 1), 1)
            for i, _ in enumerate(prep_gen):
                if todo and (i + 1) % every == 0:
                    holder[0] = scan_step(slot, todo.pop(0), holder[0], rows_out, o_ref, has_q)
        for t in todo:
            holder[0] = scan_step(slot, t, holder[0], rows_out, o_ref, has_q)
        return holder[0]

    for o_ref in outs:
        o_ref[...] = jnp.zeros_like(o_ref)

    n_all = n_ctx + n_lat
    zero = jnp.zeros((DN_DK, DN_DK), F32)
    states = tuple(zero for _ in chains)

    def ctx_pos(p):
        return (p, n_ctx - 1 - p)

    def lat_pos(p):
        return (p, n_lat - 1 - p)

    def rows_of(pos):
        return [pos[0] * sup, pos[1] * sup]

    o_ctx = outs[1] if ctx_q else None
    run(prep(0, ctx_in, ctx_pos(0), ctx_q), None, states, None, None, False)
    for p in range(n_ctx):
        if p + 1 < n_ctx:
            nxt = prep((p + 1) % 2, ctx_in, ctx_pos(p + 1), ctx_q)
        else:
            nxt = prep((p + 1) % 2, lat_in, lat_pos(0), True)
        states = run(nxt, p % 2, states, rows_of(ctx_pos(p)), o_ctx, ctx_q)

    def lat_body(p, states):
        slot = (n_ctx + p) % 2
        nxt = prep(1 - slot, lat_in, lat_pos(p + 1), True)
        return run(nxt, slot, states, rows_of(lat_pos(p)), outs[0], True)

    states = lax.fori_loop(0, n_lat - 1, lat_body, states)
    run(None, (n_all - 1) % 2, states, rows_of(lat_pos(n_lat - 1)), outs[0], True)


def _dn_call(ctx_in, lat_in, *, chunk, sup, hp, ctx_q):
    bsz, t_lat, _ = lat_in[0].shape
    t_ctx = ctx_in[0].shape[1]
    assert t_ctx % sup == 0 and t_lat % sup == 0 and sup % chunk == 0 and DN_HEADS % hp == 0
    n_ctx, n_lat = t_ctx // sup, t_lat // sup
    n_rows = 2 * DN_DIRS * DN_HEADS
    nb = sup // chunk

    def specs(t_len, has_q):
        head_blk = pl.BlockSpec((1, t_len, hp * DN_DK), lambda b, h: (b, 0, h))
        sp = [head_blk, head_blk,
              pl.BlockSpec((1, t_len, LANES), lambda b, h: (b, 0, 0)),
              pl.BlockSpec((1, n_rows, t_len), lambda b, h: (b, 0, 0))]
        if has_q:
            sp.append(head_blk)
        return sp

    out_shape = [jax.ShapeDtypeStruct((bsz, t_lat, DN_W), F32)]
    out_specs = [pl.BlockSpec((1, t_lat, hp * DN_DK), lambda b, h: (b, 0, h))]
    if ctx_q:
        out_shape.append(jax.ShapeDtypeStruct((bsz, t_ctx, DN_W), F32))
        out_specs.append(pl.BlockSpec((1, t_ctx, hp * DN_DK), lambda b, h: (b, 0, h)))
    nch = hp * DN_DIRS
    scratch = [
        pltpu.VMEM((2, nch, sup, DN_DK), F32),
        pltpu.VMEM((2, nch, sup, DN_DK), BF16),
        pltpu.VMEM((2, nch, sup, DN_DK), BF16),
        pltpu.VMEM((2, nch, sup, DN_DK), BF16),
        pltpu.VMEM((2, nch, sup, chunk), BF16),
        pltpu.VMEM((2, nch, nb, HALO, LANES), F32),
    ]
    kern = functools.partial(_dn_kernel, chunk=chunk, sup=sup, n_ctx=n_ctx, n_lat=n_lat, ctx_q=ctx_q, hp=hp)
    return pl.pallas_call(
        kern,
        out_shape=out_shape,
        grid=(bsz, DN_HEADS // hp),
        in_specs=specs(t_ctx, ctx_q) + specs(t_lat, True),
        out_specs=out_specs,
        scratch_shapes=scratch,
        compiler_params=pltpu.CompilerParams(
            dimension_semantics=("parallel", "arbitrary"), vmem_limit_bytes=VMEM_LIMIT),
        name="dn_q" if ctx_q else "dn_kv",
    )(*ctx_in, *lat_in)


def _attn_kernel(*refs, n_sets):
    q_ref = refs[0]
    kv = refs[1:1 + 2 * n_sets]
    o_ref = refs[1 + 2 * n_sets]
    for g in range(ATT_KV_HEADS):
        for hh in range(ATT_GROUPS):
            h0 = (g * ATT_GROUPS + hh) * ATT_HD
            q = q_ref[0, :, h0:h0 + ATT_HD]
            scores = [_dot(q, kv[2 * i][0, g * ATT_HD:(g + 1) * ATT_HD, :]) for i in range(n_sets)]
            m = scores[0].max(axis=-1, keepdims=True)
            for s in scores[1:]:
                m = jnp.maximum(m, s.max(axis=-1, keepdims=True))
            den = 0.0
            acc = 0.0
            for i, s in enumerate(scores):
                p = jnp.exp(s - m)
                den = den + jnp.sum(p, axis=-1, keepdims=True)
                acc = acc + _dot(p.astype(BF16), kv[2 * i + 1][0])
            o_ref[0, :, h0:h0 + ATT_HD] = (acc[:, g * ATT_HD:(g + 1) * ATT_HD] / den).astype(BF16)


def _attn_call(q, kv_sets, *, tq):
    bsz, t_len, _ = q.shape
    tq = min(tq, t_len)
    in_specs = [pl.BlockSpec((1, tq, ATT_Q_W), lambda b, t: (b, t, 0))]
    args = [q]
    for k_t, v in kv_sets:
        s_len = v.shape[1]
        in_specs += [pl.BlockSpec((1, ATT_KV_W, s_len), lambda b, t: (b, 0, 0)),
                     pl.BlockSpec((1, s_len, ATT_KV_W), lambda b, t: (b, 0, 0))]
        args += [k_t, v]
    return pl.pallas_call(
        functools.partial(_attn_kernel, n_sets=len(kv_sets)),
        out_shape=jax.ShapeDtypeStruct((bsz, t_len, ATT_Q_W), BF16),
        grid=(bsz, t_len // tq),
        in_specs=in_specs,
        out_specs=pl.BlockSpec((1, tq, ATT_Q_W), lambda b, t: (b, t, 0)),
        compiler_params=pltpu.CompilerParams(
            dimension_semantics=("parallel", "arbitrary"), vmem_limit_bytes=VMEM_LIMIT),
        name="attn%d" % len(kv_sets),
    )(*args)


def _mixout_kernel(x_ref, o_ref, z_ref, b_ref, mod_ref, dnn_ref, npost_ref, w_ref, out_ref):
    gate = mod_ref[0, 2:3, :]
    o = o_ref[0]
    z = z_ref[0]
    parts = []
    for hd in range(DN_HEADS):
        seg = o[:, hd * DN_DK:(hd + 1) * DN_DK]
        parts.append(_rms(seg, dnn_ref[...]) * _silu(z[:, hd * DN_DK:(hd + 1) * DN_DK]))
    a = jnp.concatenate(parts, axis=1).astype(BF16)
    y = _dot(a, w_ref[0:DN_W, :]) + _dot(b_ref[0], w_ref[DN_W:, :])
    out_ref[0] = x_ref[0] + gate * _rms(y, npost_ref[...])


def _mixout_call(x, o_dn, z, b_att, mod, dnn, npost, w_out, *, tm):
    bsz, t_len, d = x.shape
    tm = min(tm, t_len)
    row = lambda n: pl.BlockSpec((1, tm, n), lambda b, t: (b, t, 0))
    return pl.pallas_call(
        _mixout_kernel,
        out_shape=jax.ShapeDtypeStruct(x.shape, F32),
        grid=(bsz, t_len // tm),
        in_specs=[row(d), row(DN_W), row(DN_W), row(ATT_Q_W), _mod_spec(mod),
                  pl.BlockSpec((1, DN_DK), lambda b, t: (0, 0)),
                  pl.BlockSpec((1, d), lambda b, t: (0, 0)),
                  _resident(w_out.shape)],
        out_specs=row(d),
        compiler_params=pltpu.CompilerParams(
            dimension_semantics=("parallel", "arbitrary"), vmem_limit_bytes=VMEM_LIMIT),
        name="mixout",
    )(x, o_dn, z, b_att, mod, dnn, npost, w_out)


def _rope_tables(n):
    half = ATT_HD // 2
    r = (jnp.arange(n) // GRID_W).astype(F32)
    col = (jnp.arange(n) % GRID_W).astype(F32)
    inv = ROPE_THETA ** (-jnp.arange(0, half, 2, dtype=F32) / half)
    ang = jnp.concatenate([r[:, None] * inv, col[:, None] * inv], axis=-1)
    cos, sin = jnp.cos(ang), jnp.sin(ang)
    cos_t = jnp.tile(jnp.concatenate([cos, cos], axis=-1), (1, LANES // ATT_HD))
    sin_t = jnp.tile(jnp.concatenate([-sin, sin], axis=-1), (1, LANES // ATT_HD))
    return cos_t, sin_t


def _even_weights(w_in, q_norm, k_norm, a_log, dt_bias):
    kv_cols = 2 * DN_W + 2 * DN_DIRS * DN_HEADS + 2 * ATT_KV_W
    perm = np.concatenate([np.arange(0, ATT_HD, 2), np.arange(1, ATT_HD, 2)])
    off_ab = 2 * DN_W
    off_ak = off_ab + 2 * DN_DIRS * DN_HEADS
    off_av = off_ak + ATT_KV_W
    off_q = kv_cols
    off_z = off_q + DN_W
    off_aq = off_z + DN_W
    aq_idx = np.concatenate([off_aq + h * ATT_HD + perm for h in range(ATT_HEADS)])
    ak_idx = np.concatenate([off_ak + h * ATT_HD + perm for h in range(ATT_KV_HEADS)])
    cols = np.concatenate([
        np.arange(0, 2 * DN_W), np.arange(off_q, off_q + DN_W), np.arange(off_z, off_z + DN_W),
        aq_idx, ak_idx, np.arange(off_av, off_av + ATT_KV_W), np.arange(off_ab, off_ab + 2 * DN_DIRS * DN_HEADS)])
    w = jnp.take(w_in, jnp.asarray(cols), axis=1)
    w = jnp.pad(w, ((0, 0), (0, PC_END - w.shape[1]))).astype(BF16)
    tile2 = lambda v: jnp.tile(v[perm], LANES // ATT_HD).reshape(1, LANES)
    pad = lambda v: jnp.pad(v.reshape(-1), (0, LANES - v.size)).reshape(1, LANES)
    return w, tile2(q_norm), tile2(k_norm), pad(a_log), pad(dt_bias)


def kernel(x, c, ctx, c_ctx, mod_w, mod_b, norm_mix_pre, norm_mix_post, norm_ffn_pre, norm_ffn_post,
           hyb_w_in, hyb_w_out, dn_conv_w, dn_a_log, dn_dt_bias, dn_out_norm, att_q_norm, att_k_norm,
           sc_w_in, sc_conv_w, sc_w_out, ffn_w_up, ffn_conv_w, ffn_w_down):
    bsz, seq, d = x.shape
    depth = mod_w.shape[0]
    ctx_len = ctx.shape[1]
    fc = 256
    tm_glu, tm_proj, tm_out, tq = 512, 256, 512, 256

    rows = ((bsz + 1 + HALO - 1) // HALO) * HALO
    s_pad = jnp.zeros((rows, d), F32).at[:bsz].set(c).at[bsz].set(c_ctx)
    mod_all = _mod_call(s_pad, mod_w, mod_b).reshape(depth, rows, 6, d)

    cos_l, sin_l = _rope_tables(seq)
    cos_c, sin_c = jnp.ones((ctx_len, LANES), F32), jnp.zeros((ctx_len, LANES), F32)
    vec = lambda v: v.reshape(1, -1)

    for layer in range(depth):
        even = layer % 2 == 0
        ctx_live = any(j % 2 == 0 for j in range(layer + 1, depth))
        mod_l = mod_all[layer, :bsz]
        mod_c = mod_all[layer, bsz:bsz + 1]
        npre, npost = vec(norm_mix_pre[layer]), vec(norm_mix_post[layer])
        if even:
            e = layer // 2
            w, qn, kn, alog, dtb = _even_weights(hyb_w_in[e], att_q_norm[e], att_k_norm[e],
                                                 dn_a_log[e], dn_dt_bias[e])
            cw = dn_conv_w[e]
            w_out = hyb_w_out[e].astype(BF16)
            dnn = vec(dn_out_norm[e])
            proj = functools.partial(_proj_call, npre=npre, w=w, cw=cw, alog=alog, dtb=dtb, qn=qn, kn=kn,
                                     tm=tm_proj, chunk=DN_CHUNK)
            pl_ = proj(x, mod_l, cos_t=cos_l, sin_t=sin_l, q_side=True)
            pc_ = proj(ctx, mod_c, cos_t=cos_c, sin_t=sin_c, q_side=ctx_live)
            k_l, v_l, gc_l, gct_l, akt_l, av_l, q_l, z_l, aq_l = pl_
            k_c, v_c, gc_c, gct_c, akt_c, av_c = pc_[:6]
            lat_in = (k_l, v_l, gc_l, gct_l, q_l)
            ctx_in = (k_c, v_c, gc_c, gct_c) + ((pc_[6],) if ctx_live else ())
            o_dn = _dn_call(ctx_in, lat_in, chunk=DN_CHUNK, sup=DN_SUPER, hp=DN_HEADS_PER_STEP, ctx_q=ctx_live)
            b_lat = _attn_call(aq_l, [(akt_c, av_c), (akt_l, av_l)], tq=tq)
            x_new = _mixout_call(x, o_dn[0], z_l, b_lat, mod_l, dnn, npost, w_out, tm=tm_out)
            if ctx_live:
                b_ctx = _attn_call(pc_[8], [(akt_c, av_c)], tq=tq)
                ctx = _mixout_call(ctx, o_dn[1], pc_[7], b_ctx, mod_c, dnn, npost, w_out, tm=tm_out)
            x = x_new
        else:
            o = layer // 2
            win = _chunk_cols(sc_w_in[o], 3, fc).astype(BF16)
            cwc = _chunk_conv(sc_conv_w[o], fc)
            wout = sc_w_out[o].reshape(-1, fc, d).astype(BF16)
            x = _glu_call(x, mod_l, npre, npost, win, cwc, wout, mode="sc", rows=(0, 1, 2), tm=tm_glu)
            if ctx_live:
                ctx = _glu_call(ctx, mod_c, npre, npost, win, cwc, wout, mode="sc", rows=(0, 1, 2), tm=tm_glu)
        win = _chunk_cols(ffn_w_up[layer], 2, fc).astype(BF16)
        cwc = _chunk_conv(ffn_conv_w[layer], fc)
        wout = ffn_w_down[layer].reshape(-1, fc, d).astype(BF16)
        npre, npost = vec(norm_ffn_pre[layer]), vec(norm_ffn_post[layer])
        x = _glu_call(x, mod_l, npre, npost, win, cwc, wout, mode="ffn", rows=(3, 4, 5), tm=tm_glu)
        if ctx_live:
            ctx = _glu_call(ctx, mod_c, npre, npost, win, cwc, wout, mode="ffn", rows=(3, 4, 5), tm=tm_glu)
    return x
```

```python
import functools

import jax
import jax.numpy as jnp
from jax import lax
from jax.experimental import pallas as pl
from jax.experimental.pallas import tpu as pltpu

F32 = jnp.float32
BF16 = jnp.bfloat16
EPS = 1e-6
HALO = 8
LANES = 128
LOG2E = 1.4426950408889634

D_MODEL = 1024
GRID_W = 64
ROPE_THETA = 10000.0
DN_HEADS = 4
DN_DK = 128
DN_DIRS = 2
DN_CHUNK = 64
DN_SUPER = 256
DN_HEADS_PER_STEP = 4
ATT_HEADS = 8
ATT_KV_HEADS = 2
ATT_HD = 64
ATT_GROUPS = ATT_HEADS // ATT_KV_HEADS
D_FF = 2816
DN_W = DN_HEADS * DN_DK
ATT_Q_W = ATT_HEADS * ATT_HD
ATT_KV_W = ATT_KV_HEADS * ATT_HD

PC_K, PC_V, PC_Q, PC_Z, PC_AQ, PC_AK, PC_AV, PC_AB, PC_END = (
    0, 512, 1024, 1536, 2048, 2560, 2688, 2816, 2944)

VMEM_LIMIT = 56 * 1024 * 1024


def _resident(shape):
    nd = len(shape)
    return pl.BlockSpec(shape, lambda *_: (0,) * nd, pipeline_mode=pl.Buffered(1))


def _rms(x, w):
    return x * lax.rsqrt(jnp.mean(x * x, axis=-1, keepdims=True) + EPS) * w


def _silu(x):
    return x * (1.0 / (1.0 + jnp.exp(-x)))


def _dot(a, b):
    return jnp.dot(a, b, preferred_element_type=F32)


def _dot_nt(a, b):
    return lax.dot_general(a, b, (((1,), (1,)), ((), ())), preferred_element_type=F32)


def _dot_tn(a, b):
    return lax.dot_general(a, b, (((0,), (0,)), ((), ())), preferred_element_type=F32)


def _split3(x):
    hi = x.astype(BF16)
    r = x - hi.astype(F32)
    mid = r.astype(BF16)
    lo = (r - mid.astype(F32)).astype(BF16)
    return hi, mid, lo


def _valid_rows(tm):
    t = pl.program_id(1)
    nt = pl.num_programs(1)
    r = lax.broadcasted_iota(jnp.int32, (tm + 2 * HALO, 1), 0)
    lo_ok = jnp.logical_or(r >= HALO, t > 0)
    hi_ok = jnp.logical_or(r < tm + HALO, t < nt - 1)
    return jnp.logical_and(lo_ok, hi_ok)


def _conv3(cin, cw, tm):
    return (cw[0:1] * cin[HALO - 1:HALO - 1 + tm] + cw[1:2] * cin[HALO:HALO + tm]
            + cw[2:3] * cin[HALO + 1:HALO + 1 + tm])


def _halo_specs(tm, t_len, d):
    tb = tm // HALO
    nhb = t_len // HALO
    return [
        pl.BlockSpec((1, tm, d), lambda b, t: (b, t, 0)),
        pl.BlockSpec((1, HALO, d), lambda b, t: (b, jnp.maximum(t * tb - 1, 0), 0)),
        pl.BlockSpec((1, HALO, d), lambda b, t: (b, jnp.minimum((t + 1) * tb, nhb - 1), 0)),
    ]


def _mod_spec(mod):
    if mod.shape[0] > 1:
        return pl.BlockSpec((1, 6, mod.shape[2]), lambda b, t: (b, 0, 0))
    return pl.BlockSpec((1, 6, mod.shape[2]), lambda b, t: (0, 0, 0))


def _mod_kernel(s_ref, w_ref, b_ref, o_ref):
    s = _silu(s_ref[...])
    o_ref[0] = jnp.dot(s, w_ref[0], preferred_element_type=F32,
                       precision=lax.Precision.HIGHEST) + b_ref[0]


def _mod_call(s_pad, mod_w, mod_b):
    depth, d, n6 = mod_w.shape
    rows = s_pad.shape[0]
    nb = n6 // d
    return pl.pallas_call(
        _mod_kernel,
        out_shape=jax.ShapeDtypeStruct((depth, rows, n6), F32),
        grid=(depth, nb),
        in_specs=[
            pl.BlockSpec((rows, d), lambda l, j: (0, 0)),
            pl.BlockSpec((1, d, d), lambda l, j: (l, 0, j)),
            pl.BlockSpec((1, 1, d), lambda l, j: (l, 0, j)),
        ],
        out_specs=pl.BlockSpec((1, rows, d), lambda l, j: (l, 0, j)),
        compiler_params=pltpu.CompilerParams(
            dimension_semantics=("arbitrary", "arbitrary"), vmem_limit_bytes=VMEM_LIMIT),
        name="mod",
    )(s_pad, mod_w, mod_b.reshape(depth, 1, n6))


def _glu_kernel(xm_ref, xp_ref, xn_ref, mod_ref, npre_ref, npost_ref, win_ref, cw_ref, wout_ref,
                o_ref, acc_ref, *, mode, tm, nf, fc, rows):
    sh_i, sc_i, gt_i = rows
    x_main = xm_ref[0]
    x_ext = jnp.concatenate([xp_ref[0], x_main, xn_ref[0]], axis=0)
    shift = mod_ref[0, sh_i:sh_i + 1, :]
    scale = mod_ref[0, sc_i:sc_i + 1, :]
    gate = mod_ref[0, gt_i:gt_i + 1, :]
    hb = (_rms(x_ext, npre_ref[...]) * (1.0 + scale) + shift).astype(BF16)
    valid = _valid_rows(tm)
    nbr = 2 if mode == "ffn" else 3
    f_all = nf * fc

    def up(j):
        return [_dot(hb, win_ref[:, b * f_all + j * fc:b * f_all + (j + 1) * fc]) for b in range(nbr)]

    def unit(j, p):
        cw = cw_ref[:, j * fc:(j + 1) * fc]
        if mode == "ffn":
            conv = _conv3(jnp.where(valid, p[0], 0.0), cw, tm)
            g = _silu(conv) * p[1][HALO:HALO + tm]
        else:
            conv = _conv3(jnp.where(valid, p[1] * p[2], 0.0), cw, tm)
            g = p[0][HALO:HALO + tm] * conv
        return g.astype(BF16)

    def down(j, g):
        y = _dot(g, wout_ref[j * fc:(j + 1) * fc, :])
        if j == 0:
            acc_ref[...] = y
        else:
            acc_ref[...] += y

    depth = 3 if nf > 8 else 0
    p = up(0)
    pending = []
    for j in range(nf):
        p_next = up(j + 1) if j + 1 < nf else None
        if depth and len(pending) >= depth:
            down(*pending.pop(0))
        pending.append((j, unit(j, p)))
        if not depth:
            down(*pending.pop(0))
        p = p_next
    for item in pending:
        down(*item)
    o_ref[0] = x_main + gate * _rms(acc_ref[...], npost_ref[...])


def _glu_call(x, mod, npre, npost, win, cw, wout, *, mode, rows, tm, fc):
    bsz, t_len, d = x.shape
    f_all = wout.shape[0]
    assert f_all % fc == 0 and fc % LANES == 0
    nf = f_all // fc
    tm = min(tm, t_len)
    kern = functools.partial(_glu_kernel, mode=mode, tm=tm, nf=nf, fc=fc, rows=rows)
    return pl.pallas_call(
        kern,
        out_shape=jax.ShapeDtypeStruct(x.shape, F32),
        grid=(bsz, t_len // tm),
        in_specs=_halo_specs(tm, t_len, d) + [
            _mod_spec(mod),
            pl.BlockSpec((1, d), lambda b, t: (0, 0)),
            pl.BlockSpec((1, d), lambda b, t: (0, 0)),
            _resident(win.shape), _resident(cw.shape), _resident(wout.shape),
        ],
        out_specs=pl.BlockSpec((1, tm, d), lambda b, t: (b, t, 0)),
        scratch_shapes=[pltpu.VMEM((tm, d), F32)],
        compiler_params=pltpu.CompilerParams(
            dimension_semantics=("parallel", "arbitrary"), vmem_limit_bytes=VMEM_LIMIT),
        name="glu_" + mode,
    )(x, x, x, mod, npre, npost, win, cw, wout)


def _proj_kernel(*refs, tm, chunk, q_side):
    (xm_ref, xp_ref, xn_ref, mod_ref, npre_ref, w_ref, cw_ref, alog_ref, dtb_ref,
     qn_ref, kn_ref, cos_ref, sin_ref) = refs[:13]
    outs = refs[13:]
    k_o, v_o, gc_o, gcT_o, ak_o, avT_o = outs[:6]
    if q_side:
        q_o, z_o, aq_o = outs[6:9]

    x_main = xm_ref[0]
    x_ext = jnp.concatenate([xp_ref[0], x_main, xn_ref[0]], axis=0)
    shift = mod_ref[0, 0:1, :]
    scale = mod_ref[0, 1:2, :]
    npre = npre_ref[...]
    hb_ext = (_rms(x_ext, npre) * (1.0 + scale) + shift).astype(BF16)
    hb = (_rms(x_main, npre) * (1.0 + scale) + shift).astype(BF16)
    valid = _valid_rows(tm)

    groups = [(PC_K, k_o, True, 1.0), (PC_V, v_o, False, 1.0)]
    if q_side:
        groups.append((PC_Q, q_o, True, DN_DK ** -0.5))
    for c0, o_ref, unit, mult in groups:
        p = jnp.where(valid, _dot(hb_ext, w_ref[:, c0:c0 + DN_W]), 0.0)
        a = _silu(_conv3(p, cw_ref[:, c0:c0 + DN_W], tm))
        if unit:
            for hd in range(DN_HEADS):
                seg = a[:, hd * DN_DK:(hd + 1) * DN_DK]
                inv = lax.rsqrt(jnp.sum(seg * seg, axis=-1, keepdims=True) + EPS)
                o_ref[0, :, hd * DN_DK:(hd + 1) * DN_DK] = (seg * (inv * mult)).astype(BF16)
        else:
            o_ref[0] = a.astype(BF16)

    lane = lax.broadcasted_iota(jnp.int32, (1, LANES), 1)
    half = (lane % ATT_HD) < (ATT_HD // 2)
    cos_t = cos_ref[...]
    sin_t = sin_ref[...]
    r_i = lax.broadcasted_iota(jnp.int32, (LANES, LANES), 0)
    c_i = lax.broadcasted_iota(jnp.int32, (LANES, LANES), 1)
    head_mean = jnp.where((r_i // ATT_HD) == (c_i // ATT_HD), 1.0 / ATT_HD, 0.0).astype(BF16)

    def head_norm_rope(xa, wn):
        hi, mid, lo = _split3(xa * xa)
        ms = _dot(hi, head_mean) + _dot(mid, head_mean) + _dot(lo, head_mean)
        xa = xa * lax.rsqrt(ms + EPS) * wn
        swapped = jnp.where(half, pltpu.roll(xa, LANES - ATT_HD // 2, 1), pltpu.roll(xa, ATT_HD // 2, 1))
        return xa * cos_t + swapped * sin_t

    if q_side:
        z_o[0] = _dot(hb, w_ref[:, PC_Z:PC_AQ])
        pq = _dot(hb, w_ref[:, PC_AQ:PC_AK])
        for gq in range(ATT_Q_W // LANES):
            xa = head_norm_rope(pq[:, gq * LANES:(gq + 1) * LANES], qn_ref[...])
            aq_o[0, gq * LANES:(gq + 1) * LANES, :] = (xa * (ATT_HD ** -0.5 * LOG2E)).T.astype(BF16)

    pk = _dot(hb, w_ref[:, PC_AK:PC_END])
    ak = head_norm_rope(pk[:, :LANES], kn_ref[...])
    ak_o[0] = ak.astype(BF16)
    avT_o[0] = pk[:, LANES:2 * LANES].T.astype(BF16)

    pab = pk[:, 2 * LANES:]
    n_dh = DN_DIRS * DN_HEADS
    zab = pab + dtb_ref[...]
    softplus = jnp.maximum(zab, 0.0) + jnp.log(1.0 + jnp.exp(-jnp.abs(zab)))
    g = -jnp.exp(alog_ref[...]) * softplus
    beta = 1.0 / (1.0 + jnp.exp(-pab))
    g = jnp.where(lane < n_dh, g, 0.0)
    rr = lax.broadcasted_iota(jnp.int32, (tm, tm), 0)
    cc = lax.broadcasted_iota(jnp.int32, (tm, tm), 1)
    same = (rr // chunk) == (cc // chunk)
    low = jnp.where(jnp.logical_and(same, cc <= rr), 1.0, 0.0).astype(BF16)
    upp = jnp.where(jnp.logical_and(same, cc >= rr), 1.0, 0.0).astype(BF16)
    hi, mid, lo = _split3(g)
    pre = _dot(low, hi) + _dot(low, mid) + _dot(low, lo)
    suf = _dot(upp, hi) + _dot(upp, mid) + _dot(upp, lo)
    gc = jnp.where(lane < DN_HEADS, pre, jnp.where(lane < n_dh, suf, jnp.where(lane < 2 * n_dh, beta, 0.0)))
    gc_o[0] = gc
    gcT_o[0] = gc.T[0:2 * n_dh, :]


def _proj_call(x, mod, npre, w, cw, alog, dtb, qn, kn, cos_t, sin_t, *, q_side, tm, chunk):
    bsz, t_len, d = x.shape
    tm = min(tm, t_len)
    nt = t_len // tm
    f32_out = lambda n: jax.ShapeDtypeStruct((bsz, t_len, n), F32)
    bf16_out = lambda n: jax.ShapeDtypeStruct((bsz, t_len, n), BF16)
    row_spec = lambda n: pl.BlockSpec((1, tm, n), lambda b, t: (b, t, 0))
    out_shape = [bf16_out(DN_W), bf16_out(DN_W), f32_out(LANES),
                 jax.ShapeDtypeStruct((bsz, 2 * DN_DIRS * DN_HEADS, t_len), F32),
                 bf16_out(ATT_KV_W),
                 jax.ShapeDtypeStruct((bsz, ATT_KV_W, t_len), BF16)]
    out_specs = [row_spec(DN_W), row_spec(DN_W), row_spec(LANES),
                 pl.BlockSpec((1, 2 * DN_DIRS * DN_HEADS, tm), lambda b, t: (b, 0, t)),
                 row_spec(ATT_KV_W),
                 pl.BlockSpec((1, ATT_KV_W, tm), lambda b, t: (b, 0, t))]
    if q_side:
        out_shape += [bf16_out(DN_W), f32_out(DN_W), jax.ShapeDtypeStruct((bsz, ATT_Q_W, t_len), BF16)]
        out_specs += [row_spec(DN_W), row_spec(DN_W), pl.BlockSpec((1, ATT_Q_W, tm), lambda b, t: (b, 0, t))]
    vec = lambda n: pl.BlockSpec((1, n), lambda b, t: (0, 0))
    kern = functools.partial(_proj_kernel, tm=tm, chunk=chunk, q_side=q_side)
    return pl.pallas_call(
        kern,
        out_shape=out_shape,
        grid=(bsz, nt),
        in_specs=_halo_specs(tm, t_len, d) + [
            _mod_spec(mod), vec(d), _resident(w.shape), _resident(cw.shape),
            vec(LANES), vec(LANES), vec(LANES), vec(LANES),
            pl.BlockSpec((tm, LANES), lambda b, t: (t, 0)),
            pl.BlockSpec((tm, LANES), lambda b, t: (t, 0)),
        ],
        out_specs=out_specs,
        compiler_params=pltpu.CompilerParams(
            dimension_semantics=("parallel", "arbitrary"), vmem_limit_bytes=VMEM_LIMIT),
        name="proj_q" if q_side else "proj_kv",
    )(x, x, x, mod, npre, w, cw, alog, dtb, qn, kn, cos_t, sin_t)


def _dn_kernel(*refs, chunk, sup, n_ctx, n_lat, ctx_q, hp):
    c = chunk
    nb = sup // c
    n_in_ctx = 5 if ctx_q else 4
    ctx_in = refs[:n_in_ctx]
    lat_in = refs[n_in_ctx:n_in_ctx + 5]
    n_out = 2 if ctx_q else 1
    outs = refs[n_in_ctx + 5:n_in_ctx + 5 + n_out]
    u_s, w_s, kd_s, qg_s, in_s, eg_s = refs[n_in_ctx + 5 + n_out:]
    head0 = pl.program_id(1) * hp
    chains = [(hh, d) for hh in range(hp) for d in range(DN_DIRS)]
    n_dh = DN_DIRS * DN_HEADS

    ri = lax.broadcasted_iota(jnp.int32, (sup, sup), 0)
    ci = lax.broadcasted_iota(jnp.int32, (sup, sup), 1)
    same = (ri // c) == (ci // c)
    incl = [jnp.logical_and(same, ci <= ri), jnp.logical_and(same, ci >= ri)]
    strict = [jnp.logical_and(same, ci < ri), jnp.logical_and(same, ci > ri)]
    rw = lax.broadcasted_iota(jnp.int32, (c, sup), 0)
    cw = lax.broadcasted_iota(jnp.int32, (c, sup), 1)
    cl = cw % c
    lane_blk = cw // c
    eye_w = jnp.where(rw == cl, 1.0, 0.0)
    level_mask = []
    n = 1
    while n < c:
        level_mask.append(jnp.logical_and(rw // (2 * n) == cl // (2 * n), rw // n != cl // n))
        n *= 2
    lane = lax.broadcasted_iota(jnp.int32, (1, LANES), 1)
    sub = lax.broadcasted_iota(jnp.int32, (2 * n_dh, 1), 0)

    def pick_col(x, idx):
        return jnp.sum(jnp.where(lane == idx, x, 0.0), axis=-1, keepdims=True)

    def block_diag(xw):
        return jnp.where(same, jnp.concatenate([xw] * nb, axis=0), 0.0).astype(BF16)

    def prep(slot, src, pos, has_q):
        if has_q:
            k_ref, v_ref, gc_ref, gct_ref, q_ref = src
        else:
            k_ref, v_ref, gc_ref, gct_ref = src
        r0 = [pl.multiple_of(pos[d] * sup, sup) for d in range(DN_DIRS)]
        gcb = [gc_ref[0, pl.ds(r0[d], sup), :] for d in range(DN_DIRS)]
        gct = [gct_ref[0, :, pl.ds(r0[d], sup)] for d in range(DN_DIRS)]
        st = []
        for n, (hh, d) in enumerate(chains):
            hs = slice(hh * DN_DK, (hh + 1) * DN_DK)
            k16 = k_ref[0, pl.ds(r0[d], sup), hs]
            k = k16.astype(F32)
            v = v_ref[0, pl.ds(r0[d], sup), hs].astype(F32)
            q = q_ref[0, pl.ds(r0[d], sup), hs].astype(F32) if has_q else None
            col = d * DN_HEADS + head0 + hh
            gc_col = pick_col(gcb[d], col)
            beta = pick_col(gcb[d], n_dh + col)
            gc_row = jnp.sum(jnp.where(sub == col, gct[d], 0.0), axis=0, keepdims=True)
            decay = jnp.where(incl[d], jnp.exp(jnp.where(incl[d], gc_col - gc_row, 0.0)), 0.0)
            kbeta = k * beta
            if has_q:
                kk = _dot_nt(jnp.concatenate([kbeta, q], axis=0).astype(BF16), k16)
                low = jnp.where(strict[d], kk[:sup] * decay, 0.0)
                intra = kk[sup:] * decay
                for b in range(nb):
                    in_s[slot, n, b * c:(b + 1) * c, :] = intra[b * c:(b + 1) * c, b * c:(b + 1) * c].astype(BF16)
            else:
                low = jnp.where(strict[d], _dot_nt(kbeta.astype(BF16), k16) * decay, 0.0)
            low_w = jnp.where(lane_blk == 0, low[0:c], 0.0)
            for b in range(1, nb):
                low_w = low_w + jnp.where(lane_blk == b, low[b * c:(b + 1) * c], 0.0)
            egc = jnp.exp(gc_col)
            rhs = jnp.concatenate([v * beta, kbeta * egc], axis=1).astype(BF16)
            last = [b * c + (c - 1 if d == 0 else 0) for b in range(nb)]
            g_last = jnp.concatenate([jnp.broadcast_to(gc_col[i:i + 1], (c, 1)) for i in last], axis=0)
            kd_s[slot, n] = (k * jnp.exp(g_last - gc_col)).astype(BF16)
            for b, i in enumerate(last):
                eg_s[slot, n, b] = jnp.broadcast_to(jnp.exp(gc_col[i:i + 1]), (HALO, LANES))
            if has_q:
                qg_s[slot, n] = (q * egc).astype(BF16)
            st.append((low_w, rhs))
            yield

        tw = [eye_w - jnp.where(level_mask[0], s[0], 0.0) for s in st]
        for lm in level_mask[1:]:
            et = [_dot(jnp.where(lm, s[0], 0.0).astype(BF16), block_diag(t)) for s, t in zip(st, tw)]
            yield
            tw = [t - _dot(t.astype(BF16), block_diag(e)) for t, e in zip(tw, et)]
            yield

        for n, (s, t) in enumerate(zip(st, tw)):
            uw = _dot(block_diag(t), s[1])
            u_s[slot, n] = uw[:, :DN_DK]
            w_s[slot, n] = uw[:, DN_DK:].astype(BF16)
            yield

    def scan_step(slot, t, states, rows_out, o_ref, has_q):
        new = []
        rows = [slice(t * c, (t + 1) * c), slice((nb - 1 - t) * c, (nb - t) * c)]
        blk = [t, nb - 1 - t]
        s16 = [s.astype(BF16) for s in states]
        if has_q:
            ws = [_dot(jnp.concatenate([w_s[slot, n, rows[d], :], qg_s[slot, n, rows[d], :]], axis=0), s16[n])
                  for n, (hh, d) in enumerate(chains)]
            v16 = [(u_s[slot, n, rows[d], :] - ws[n][:c]).astype(BF16) for n, (hh, d) in enumerate(chains)]
            for n, (hh, d) in enumerate(chains):
                o = ws[n][c:] + _dot(in_s[slot, n, rows[d], :], v16[n])
                dst = pl.ds(pl.multiple_of(rows_out[d] + blk[d] * c, c), c)
                o_ref[0, dst, hh * DN_DK:(hh + 1) * DN_DK] += o
        else:
            v16 = [(u_s[slot, n, rows[d], :] - _dot(w_s[slot, n, rows[d], :], s16[n])).astype(BF16)
                   for n, (hh, d) in enumerate(chains)]
        for n, (hh, d) in enumerate(chains):
            new.append(states[n] * eg_s[slot, n, blk[d], 0:1, :] + _dot_tn(kd_s[slot, n, rows[d], :], v16[n]))
        return tuple(new)

    def run(prep_gen, slot, states, rows_out, o_ref, has_q):
        holder = [states]
        todo = list(range(nb)) if slot is not None else []
        if prep_gen is not None:
            n_stage = 2 * len(chains) + 2 * (len(level_mask) - 1)
            every = max(n_stage // (nb + 1), 1)
            for i, _ in enumerate(prep_gen):
                if todo and (i + 1) % every == 0:
                    holder[0] = scan_step(slot, todo.pop(0), holder[0], rows_out, o_ref, has_q)
        for t in todo:
            holder[0] = scan_step(slot, t, holder[0], rows_out, o_ref, has_q)
        return holder[0]

    for o_ref in outs:
        o_ref[...] = jnp.zeros_like(o_ref)

    n_all = n_ctx + n_lat
    zero = jnp.zeros((DN_DK, DN_DK), F32)
    states = tuple(zero for _ in chains)

    def ctx_pos(p):
        return (p, n_ctx - 1 - p)

    def lat_pos(p):
        return (p, n_lat - 1 - p)

    def rows_of(pos):
        return [pos[0] * sup, pos[1] * sup]

    o_ctx = outs[1] if ctx_q else None
    run(prep(0, ctx_in, ctx_pos(0), ctx_q), None, states, None, None, False)
    for p in range(n_ctx):
        if p + 1 < n_ctx:
            nxt = prep((p + 1) % 2, ctx_in, ctx_pos(p + 1), ctx_q)
        else:
            nxt = prep((p + 1) % 2, lat_in, lat_pos(0), True)
        states = run(nxt, p % 2, states, rows_of(ctx_pos(p)), o_ctx, ctx_q)

    def lat_body(p, states):
        slot = (n_ctx + p) % 2
        nxt = prep(1 - slot, lat_in, lat_pos(p + 1), True)
        return run(nxt, slot, states, rows_of(lat_pos(p)), outs[0], True)

    states = lax.fori_loop(0, n_lat - 1, lat_body, states)
    run(None, (n_all - 1) % 2, states, rows_of(lat_pos(n_lat - 1)), outs[0], True)


def _dn_call(ctx_in, lat_in, *, chunk, sup, hp, ctx_q):
    bsz, t_lat, _ = lat_in[0].shape
    t_ctx = ctx_in[0].shape[1]
    assert t_ctx % sup == 0 and t_lat % sup == 0 and sup % chunk == 0 and DN_HEADS % hp == 0
    n_ctx, n_lat = t_ctx // sup, t_lat // sup
    n_rows = 2 * DN_DIRS * DN_HEADS
    nb = sup // chunk

    def specs(t_len, has_q):
        head_blk = pl.BlockSpec((1, t_len, hp * DN_DK), lambda b, h: (b, 0, h))
        sp = [head_blk, head_blk,
              pl.BlockSpec((1, t_len, LANES), lambda b, h: (b, 0, 0)),
              pl.BlockSpec((1, n_rows, t_len), lambda b, h: (b, 0, 0))]
        if has_q:
            sp.append(head_blk)
        return sp

    out_shape = [jax.ShapeDtypeStruct((bsz, t_lat, DN_W), F32)]
    out_specs = [pl.BlockSpec((1, t_lat, hp * DN_DK), lambda b, h: (b, 0, h))]
    if ctx_q:
        out_shape.append(jax.ShapeDtypeStruct((bsz, t_ctx, DN_W), F32))
        out_specs.append(pl.BlockSpec((1, t_ctx, hp * DN_DK), lambda b, h: (b, 0, h)))
    nch = hp * DN_DIRS
    scratch = [
        pltpu.VMEM((2, nch, sup, DN_DK), F32),
        pltpu.VMEM((2, nch, sup, DN_DK), BF16),
        pltpu.VMEM((2, nch, sup, DN_DK), BF16),
        pltpu.VMEM((2, nch, sup, DN_DK), BF16),
        pltpu.VMEM((2, nch, sup, chunk), BF16),
        pltpu.VMEM((2, nch, nb, HALO, LANES), F32),
    ]
    kern = functools.partial(_dn_kernel, chunk=chunk, sup=sup, n_ctx=n_ctx, n_lat=n_lat, ctx_q=ctx_q, hp=hp)
    return pl.pallas_call(
        kern,
        out_shape=out_shape,
        grid=(bsz, DN_HEADS // hp),
        in_specs=specs(t_ctx, ctx_q) + specs(t_lat, True),
        out_specs=out_specs,
        scratch_shapes=scratch,
        compiler_params=pltpu.CompilerParams(
            dimension_semantics=("parallel", "arbitrary"), vmem_limit_bytes=VMEM_LIMIT),
        name="dn_q" if ctx_q else "dn_kv",
    )(*ctx_in, *lat_in)


def _attn_kernel(*refs, n_sets):
    q_ref = refs[0]
    kv = refs[1:1 + 2 * n_sets]
    o_ref = refs[1 + 2 * n_sets]
    slab = 64

    def fold(x, op):
        acc = x[0:slab]
        for r in range(slab, x.shape[0], slab):
            acc = op(acc, x[r:r + slab])
        return acc

    def head_scores(h):
        g = h // ATT_GROUPS
        q_t = q_ref[0, h * ATT_HD:(h + 1) * ATT_HD, :]
        zero = jnp.zeros_like(q_t)
        q_pad = jnp.concatenate([q_t, zero] if g == 0 else [zero, q_t], axis=0)
        return [_dot(kv[2 * i][0], q_pad) for i in range(n_sets)]

    def head_out(h, ps, den):
        g = h // ATT_GROUPS
        acc = _dot(kv[1][0, g * ATT_HD:(g + 1) * ATT_HD, :], ps[0])
        for i in range(1, n_sets):
            acc = acc + _dot(kv[2 * i + 1][0, g * ATT_HD:(g + 1) * ATT_HD, :], ps[i])
        return acc / den

    outs = []
    scores = head_scores(0)
    prev = None
    for h in range(ATT_HEADS):
        nxt = head_scores(h + 1) if h + 1 < ATT_HEADS else None
        if prev is not None:
            outs.append(head_out(h - 1, *prev))
        m = fold(scores[0], jnp.maximum)
        for s in scores[1:]:
            m = jnp.maximum(m, fold(s, jnp.maximum))
        m = m.max(axis=0, keepdims=True)
        ps = [jnp.exp2(s - m) for s in scores]
        den = fold(ps[0], jnp.add)
        for p in ps[1:]:
            den = den + fold(p, jnp.add)
        den = den.sum(axis=0, keepdims=True)
        prev = ([p.astype(BF16) for p in ps], den)
        scores = nxt
    outs.append(head_out(ATT_HEADS - 1, *prev))
    o_ref[0] = jnp.concatenate(outs, axis=0).T.astype(BF16)


def _attn_call(q_t, kv_sets, *, tq):
    bsz, _, t_len = q_t.shape
    tq = min(tq, t_len)
    in_specs = [pl.BlockSpec((1, ATT_Q_W, tq), lambda b, t: (b, 0, t))]
    args = [q_t]
    for k, v_t in kv_sets:
        s_len = k.shape[1]
        in_specs += [pl.BlockSpec((1, s_len, ATT_KV_W), lambda b, t: (b, 0, 0)),
                     pl.BlockSpec((1, ATT_KV_W, s_len), lambda b, t: (b, 0, 0))]
        args += [k, v_t]
    return pl.pallas_call(
        functools.partial(_attn_kernel, n_sets=len(kv_sets)),
        out_shape=jax.ShapeDtypeStruct((bsz, t_len, ATT_Q_W), BF16),
        grid=(bsz, t_len // tq),
        in_specs=in_specs,
        out_specs=pl.BlockSpec((1, tq, ATT_Q_W), lambda b, t: (b, t, 0)),
        compiler_params=pltpu.CompilerParams(
            dimension_semantics=("parallel", "arbitrary"), vmem_limit_bytes=VMEM_LIMIT),
        name="attn%d" % len(kv_sets),
    )(*args)


def _mixout_kernel(x_ref, o_ref, z_ref, b_ref, mod_ref, dnn_ref, npost_ref, w_ref, out_ref):
    gate = mod_ref[0, 2:3, :]
    o = o_ref[0]
    z = z_ref[0]
    parts = []
    for hd in range(DN_HEADS):
        seg = o[:, hd * DN_DK:(hd + 1) * DN_DK]
        parts.append(_rms(seg, dnn_ref[...]) * _silu(z[:, hd * DN_DK:(hd + 1) * DN_DK]))
    a = jnp.concatenate(parts, axis=1).astype(BF16)
    y = _dot(a, w_ref[0:DN_W, :]) + _dot(b_ref[0], w_ref[DN_W:, :])
    out_ref[0] = x_ref[0] + gate * _rms(y, npost_ref[...])


def _mixout_call(x, o_dn, z, b_att, mod, dnn, npost, w_out, *, tm):
    bsz, t_len, d = x.shape
    tm = min(tm, t_len)
    row = lambda n: pl.BlockSpec((1, tm, n), lambda b, t: (b, t, 0))
    return pl.pallas_call(
        _mixout_kernel,
        out_shape=jax.ShapeDtypeStruct(x.shape, F32),
        grid=(bsz, t_len // tm),
        in_specs=[row(d), row(DN_W), row(DN_W), row(ATT_Q_W), _mod_spec(mod),
                  pl.BlockSpec((1, DN_DK), lambda b, t: (0, 0)),
                  pl.BlockSpec((1, d), lambda b, t: (0, 0)),
                  _resident(w_out.shape)],
        out_specs=row(d),
        compiler_params=pltpu.CompilerParams(
            dimension_semantics=("parallel", "arbitrary"), vmem_limit_bytes=VMEM_LIMIT),
        name="mixout",
    )(x, o_dn, z, b_att, mod, dnn, npost, w_out)


def _rope_tables(n):
    half = ATT_HD // 2
    r = (jnp.arange(n) // GRID_W).astype(F32)
    col = (jnp.arange(n) % GRID_W).astype(F32)
    inv = ROPE_THETA ** (-jnp.arange(0, half, 2, dtype=F32) / half)
    ang = jnp.concatenate([r[:, None] * inv, col[:, None] * inv], axis=-1)
    cos, sin = jnp.cos(ang), jnp.sin(ang)
    cos_t = jnp.tile(jnp.concatenate([cos, cos], axis=-1), (1, LANES // ATT_HD))
    sin_t = jnp.tile(jnp.concatenate([-sin, sin], axis=-1), (1, LANES // ATT_HD))
    return cos_t, sin_t


def _even_weights(w_in, q_norm, k_norm, a_log, dt_bias):
    d = w_in.shape[0]
    n_ab = 2 * DN_DIRS * DN_HEADS
    off_ab = 2 * DN_W
    off_ak = off_ab + n_ab
    off_av = off_ak + ATT_KV_W
    off_q = off_av + ATT_KV_W
    off_z = off_q + DN_W
    off_aq = off_z + DN_W

    def deinterleave(a, heads):
        lead = a.shape[:-1]
        a = a.reshape(lead + (heads, ATT_HD // 2, 2))
        return jnp.swapaxes(a, -1, -2).reshape(lead + (heads * ATT_HD,))

    w = jnp.concatenate([
        w_in[:, :off_ab], w_in[:, off_q:off_aq],
        deinterleave(w_in[:, off_aq:off_aq + ATT_Q_W], ATT_HEADS),
        deinterleave(w_in[:, off_ak:off_av], ATT_KV_HEADS),
        w_in[:, off_av:off_q], w_in[:, off_ab:off_ak],
        jnp.zeros((d, PC_END - PC_AB - n_ab), w_in.dtype)], axis=1).astype(BF16)
    tile2 = lambda v: jnp.tile(deinterleave(v, 1), LANES // ATT_HD).reshape(1, LANES)
    pad = lambda v: jnp.pad(v.reshape(-1), (0, LANES - v.size)).reshape(1, LANES)
    return w, tile2(q_norm), tile2(k_norm), pad(a_log), pad(dt_bias)


def kernel(x, c, ctx, c_ctx, mod_w, mod_b, norm_mix_pre, norm_mix_post, norm_ffn_pre, norm_ffn_post,
           hyb_w_in, hyb_w_out, dn_conv_w, dn_a_log, dn_dt_bias, dn_out_norm, att_q_norm, att_k_norm,
           sc_w_in, sc_conv_w, sc_w_out, ffn_w_up, ffn_conv_w, ffn_w_down):
    bsz, seq, d = x.shape
    depth = mod_w.shape[0]
    ctx_len = ctx.shape[1]
    fc = 256
    tm_glu, tm_proj, tm_out, tq = 512, 256, 512, 256

    rows = ((bsz + 1 + HALO - 1) // HALO) * HALO
    s_pad = jnp.zeros((rows, d), F32).at[:bsz].set(c).at[bsz].set(c_ctx)
    mod_all = _mod_call(s_pad, mod_w, mod_b).reshape(depth, rows, 6, d)

    cos_l, sin_l = _rope_tables(seq)
    cos_c, sin_c = jnp.ones((ctx_len, LANES), F32), jnp.zeros((ctx_len, LANES), F32)
    vec = lambda v: v.reshape(1, -1)

    for layer in range(depth):
        even = layer % 2 == 0
        ctx_live = any(j % 2 == 0 for j in range(layer + 1, depth))
        mod_l = mod_all[layer, :bsz]
        mod_c = mod_all[layer, bsz:bsz + 1]
        npre, npost = vec(norm_mix_pre[layer]), vec(norm_mix_post[layer])
        if even:
            e = layer // 2
            w, qn, kn, alog, dtb = _even_weights(hyb_w_in[e], att_q_norm[e], att_k_norm[e],
                                                 dn_a_log[e], dn_dt_bias[e])
            cw = dn_conv_w[e]
            w_out = hyb_w_out[e].astype(BF16)
            dnn = vec(dn_out_norm[e])
            proj = functools.partial(_proj_call, npre=npre, w=w, cw=cw, alog=alog, dtb=dtb, qn=qn, kn=kn,
                                     tm=tm_proj, chunk=DN_CHUNK)
            pl_ = proj(x, mod_l, cos_t=cos_l, sin_t=sin_l, q_side=True)
            pc_ = proj(ctx, mod_c, cos_t=cos_c, sin_t=sin_c, q_side=ctx_live)
            k_l, v_l, gc_l, gct_l, ak_l, avt_l, q_l, z_l, aq_l = pl_
            k_c, v_c, gc_c, gct_c, ak_c, avt_c = pc_[:6]
            lat_in = (k_l, v_l, gc_l, gct_l, q_l)
            ctx_in = (k_c, v_c, gc_c, gct_c) + ((pc_[6],) if ctx_live else ())
            o_dn = _dn_call(ctx_in, lat_in, chunk=DN_CHUNK, sup=DN_SUPER, hp=DN_HEADS_PER_STEP, ctx_q=ctx_live)
            b_lat = _attn_call(aq_l, [(ak_c, avt_c), (ak_l, avt_l)], tq=tq)
            x_new = _mixout_call(x, o_dn[0], z_l, b_lat, mod_l, dnn, npost, w_out, tm=tm_out)
            if ctx_live:
                b_ctx = _attn_call(pc_[8], [(ak_c, avt_c)], tq=tq)
                ctx = _mixout_call(ctx, o_dn[1], pc_[7], b_ctx, mod_c, dnn, npost, w_out, tm=tm_out)
            x = x_new
        else:
            o = layer // 2
            sc = functools.partial(_glu_call, npre=npre, npost=npost, win=sc_w_in[o].astype(BF16), cw=sc_conv_w[o],
                                   wout=sc_w_out[o].astype(BF16), mode="sc", rows=(0, 1, 2), tm=tm_glu, fc=fc)
            x = sc(x, mod_l)
            if ctx_live:
                ctx = sc(ctx, mod_c)
        ffn = functools.partial(_glu_call, npre=vec(norm_ffn_pre[layer]), npost=vec(norm_ffn_post[layer]),
                                win=ffn_w_up[layer].astype(BF16), cw=ffn_conv_w[layer],
                                wout=ffn_w_down[layer].astype(BF16), mode="ffn", rows=(3, 4, 5), tm=tm_glu, fc=fc)
        x = ffn(x, mod_l)
        if ctx_live:
            ctx = ffn(ctx, mod_c)
    return x
```

```python
import functools

import jax
import jax.numpy as jnp
from jax import lax
from jax.experimental import pallas as pl
from jax.experimental.pallas import tpu as pltpu

F32 = jnp.float32
BF16 = jnp.bfloat16
EPS = 1e-6
HALO = 8
LANES = 128
LOG2E = 1.4426950408889634

D_MODEL = 1024
GRID_W = 64
ROPE_THETA = 10000.0
DN_HEADS = 4
DN_DK = 128
DN_DIRS = 2
DN_CHUNK = 64
DN_SUPER = 256
DN_HEADS_PER_STEP = 4
ATT_HEADS = 8
ATT_KV_HEADS = 2
ATT_HD = 64
ATT_GROUPS = ATT_HEADS // ATT_KV_HEADS
D_FF = 2816
DN_W = DN_HEADS * DN_DK
ATT_Q_W = ATT_HEADS * ATT_HD
ATT_KV_W = ATT_KV_HEADS * ATT_HD

PC_K, PC_V, PC_Q, PC_Z, PC_AQ, PC_AK, PC_AV, PC_AB, PC_END = (
    0, 512, 1024, 1536, 2048, 2560, 2688, 2816, 2944)

VMEM_LIMIT = 56 * 1024 * 1024


def _resident(shape):
    nd = len(shape)
    return pl.BlockSpec(shape, lambda *_: (0,) * nd, pipeline_mode=pl.Buffered(1))


def _rms(x, w):
    return x * lax.rsqrt(jnp.mean(x * x, axis=-1, keepdims=True) + EPS) * w


def _silu(x):
    return x * (1.0 / (1.0 + jnp.exp(-x)))


def _dot(a, b):
    return jnp.dot(a, b, preferred_element_type=F32)


def _dot_nt(a, b):
    return lax.dot_general(a, b, (((1,), (1,)), ((), ())), preferred_element_type=F32)


def _dot_tn(a, b):
    return lax.dot_general(a, b, (((0,), (0,)), ((), ())), preferred_element_type=F32)


def _split3(x):
    hi = x.astype(BF16)
    r = x - hi.astype(F32)
    mid = r.astype(BF16)
    lo = (r - mid.astype(F32)).astype(BF16)
    return hi, mid, lo


def _valid_rows(tm):
    t = pl.program_id(1)
    nt = pl.num_programs(1)
    r = lax.broadcasted_iota(jnp.int32, (tm + 2 * HALO, 1), 0)
    lo_ok = jnp.logical_or(r >= HALO, t > 0)
    hi_ok = jnp.logical_or(r < tm + HALO, t < nt - 1)
    return jnp.logical_and(lo_ok, hi_ok)


def _conv3(cin, cw, tm):
    return (cw[0:1] * cin[HALO - 1:HALO - 1 + tm] + cw[1:2] * cin[HALO:HALO + tm]
            + cw[2:3] * cin[HALO + 1:HALO + 1 + tm])


def _halo_specs(tm, t_len, d):
    tb = tm // HALO
    nhb = t_len // HALO
    return [
        pl.BlockSpec((1, tm, d), lambda b, t: (b, t, 0)),
        pl.BlockSpec((1, HALO, d), lambda b, t: (b, jnp.maximum(t * tb - 1, 0), 0)),
        pl.BlockSpec((1, HALO, d), lambda b, t: (b, jnp.minimum((t + 1) * tb, nhb - 1), 0)),
    ]


def _mod_spec(mod):
    if mod.shape[0] > 1:
        return pl.BlockSpec((1, 6, mod.shape[2]), lambda b, t: (b, 0, 0))
    return pl.BlockSpec((1, 6, mod.shape[2]), lambda b, t: (0, 0, 0))


def _mod_kernel(s_ref, w_ref, b_ref, o_ref):
    s = _silu(s_ref[...])
    o_ref[0] = jnp.dot(s, w_ref[0], preferred_element_type=F32,
                       precision=lax.Precision.HIGHEST) + b_ref[0]


def _mod_call(s_pad, mod_w, mod_b):
    depth, d, n6 = mod_w.shape
    rows = s_pad.shape[0]
    nb = n6 // d
    return pl.pallas_call(
        _mod_kernel,
        out_shape=jax.ShapeDtypeStruct((depth, rows, n6), F32),
        grid=(depth, nb),
        in_specs=[
            pl.BlockSpec((rows, d), lambda l, j: (0, 0)),
            pl.BlockSpec((1, d, d), lambda l, j: (l, 0, j)),
            pl.BlockSpec((1, 1, d), lambda l, j: (l, 0, j)),
        ],
        out_specs=pl.BlockSpec((1, rows, d), lambda l, j: (l, 0, j)),
        compiler_params=pltpu.CompilerParams(
            dimension_semantics=("arbitrary", "arbitrary"), vmem_limit_bytes=VMEM_LIMIT),
        name="mod",
    )(s_pad, mod_w, mod_b.reshape(depth, 1, n6))


def _glu_kernel(xm_ref, xp_ref, xn_ref, mod_ref, npre_ref, npost_ref, win_ref, cw_ref, wout_ref,
                o_ref, acc_ref, *, mode, tm, nf, fc, rows):
    sh_i, sc_i, gt_i = rows
    x_main = xm_ref[0]
    x_ext = jnp.concatenate([xp_ref[0], x_main, xn_ref[0]], axis=0)
    shift = mod_ref[0, sh_i:sh_i + 1, :]
    scale = mod_ref[0, sc_i:sc_i + 1, :]
    gate = mod_ref[0, gt_i:gt_i + 1, :]
    hb = (_rms(x_ext, npre_ref[...]) * (1.0 + scale) + shift).astype(BF16)
    valid = _valid_rows(tm)
    nbr = 2 if mode == "ffn" else 3
    f_all = nf * fc

    def up(j):
        return [_dot(hb, win_ref[:, b * f_all + j * fc:b * f_all + (j + 1) * fc]) for b in range(nbr)]

    def unit(j, p):
        cw = cw_ref[:, j * fc:(j + 1) * fc]
        if mode == "ffn":
            conv = _conv3(jnp.where(valid, p[0], 0.0), cw, tm)
            g = _silu(conv) * p[1][HALO:HALO + tm]
        else:
            conv = _conv3(jnp.where(valid, p[1] * p[2], 0.0), cw, tm)
            g = p[0][HALO:HALO + tm] * conv
        return g.astype(BF16)

    def down(j, g):
        y = _dot(g, wout_ref[j * fc:(j + 1) * fc, :])
        if j == 0:
            acc_ref[...] = y
        else:
            acc_ref[...] += y

    depth = 3 if nf > 8 else 0
    p = up(0)
    pending = []
    for j in range(nf):
        p_next = up(j + 1) if j + 1 < nf else None
        if depth and len(pending) >= depth:
            down(*pending.pop(0))
        pending.append((j, unit(j, p)))
        if not depth:
            down(*pending.pop(0))
        p = p_next
    for item in pending:
        down(*item)
    o_ref[0] = x_main + gate * _rms(acc_ref[...], npost_ref[...])


def _glu_call(x, mod, npre, npost, win, cw, wout, *, mode, rows, tm, fc):
    bsz, t_len, d = x.shape
    f_all = wout.shape[0]
    assert f_all % fc == 0 and fc % LANES == 0
    nf = f_all // fc
    tm = min(tm, t_len)
    kern = functools.partial(_glu_kernel, mode=mode, tm=tm, nf=nf, fc=fc, rows=rows)
    return pl.pallas_call(
        kern,
        out_shape=jax.ShapeDtypeStruct(x.shape, F32),
        grid=(bsz, t_len // tm),
        in_specs=_halo_specs(tm, t_len, d) + [
            _mod_spec(mod),
            pl.BlockSpec((1, d), lambda b, t: (0, 0)),
            pl.BlockSpec((1, d), lambda b, t: (0, 0)),
            _resident(win.shape), _resident(cw.shape), _resident(wout.shape),
        ],
        out_specs=pl.BlockSpec((1, tm, d), lambda b, t: (b, t, 0)),
        scratch_shapes=[pltpu.VMEM((tm, d), F32)],
        compiler_params=pltpu.CompilerParams(
            dimension_semantics=("parallel", "arbitrary"), vmem_limit_bytes=VMEM_LIMIT),
        name="glu_" + mode,
    )(x, x, x, mod, npre, npost, win, cw, wout)


def _proj_kernel(*refs, tm, chunk, q_side):
    (xm_ref, xp_ref, xn_ref, mod_ref, npre_ref, w_ref, cw_ref, alog_ref, dtb_ref,
     qn_ref, kn_ref, cos_ref, sin_ref) = refs[:13]
    outs = refs[13:]
    k_o, v_o, gc_o, gcT_o, ak_o, avT_o = outs[:6]
    if q_side:
        q_o, z_o, aq_o = outs[6:9]

    x_main = xm_ref[0]
    x_ext = jnp.concatenate([xp_ref[0], x_main, xn_ref[0]], axis=0)
    shift = mod_ref[0, 0:1, :]
    scale = mod_ref[0, 1:2, :]
    npre = npre_ref[...]
    hb_ext = (_rms(x_ext, npre) * (1.0 + scale) + shift).astype(BF16)
    hb = (_rms(x_main, npre) * (1.0 + scale) + shift).astype(BF16)
    valid = _valid_rows(tm)

    groups = [(PC_K, k_o, True, 1.0), (PC_V, v_o, False, 1.0)]
    if q_side:
        groups.append((PC_Q, q_o, True, DN_DK ** -0.5))
    for c0, o_ref, unit, mult in groups:
        p = jnp.where(valid, _dot(hb_ext, w_ref[:, c0:c0 + DN_W]), 0.0)
        a = _silu(_conv3(p, cw_ref[:, c0:c0 + DN_W], tm))
        if unit:
            for hd in range(DN_HEADS):
                seg = a[:, hd * DN_DK:(hd + 1) * DN_DK]
                inv = lax.rsqrt(jnp.sum(seg * seg, axis=-1, keepdims=True) + EPS)
                o_ref[0, :, hd * DN_DK:(hd + 1) * DN_DK] = (seg * (inv * mult)).astype(BF16)
        else:
            o_ref[0] = a.astype(BF16)

    lane = lax.broadcasted_iota(jnp.int32, (1, LANES), 1)
    half = (lane % ATT_HD) < (ATT_HD // 2)
    cos_t = cos_ref[...]
    sin_t = sin_ref[...]
    r_i = lax.broadcasted_iota(jnp.int32, (LANES, LANES), 0)
    c_i = lax.broadcasted_iota(jnp.int32, (LANES, LANES), 1)
    head_mean = jnp.where((r_i // ATT_HD) == (c_i // ATT_HD), 1.0 / ATT_HD, 0.0).astype(BF16)

    def head_norm_rope(xa, wn):
        hi, mid, lo = _split3(xa * xa)
        ms = _dot(hi, head_mean) + _dot(mid, head_mean) + _dot(lo, head_mean)
        xa = xa * lax.rsqrt(ms + EPS) * wn
        swapped = jnp.where(half, pltpu.roll(xa, LANES - ATT_HD // 2, 1), pltpu.roll(xa, ATT_HD // 2, 1))
        return xa * cos_t + swapped * sin_t

    if q_side:
        z_o[0] = _dot(hb, w_ref[:, PC_Z:PC_AQ])
        pq = _dot(hb, w_ref[:, PC_AQ:PC_AK])
        for gq in range(ATT_Q_W // LANES):
            xa = head_norm_rope(pq[:, gq * LANES:(gq + 1) * LANES], qn_ref[...])
            aq_o[0, gq * LANES:(gq + 1) * LANES, :] = (xa * (ATT_HD ** -0.5 * LOG2E)).T.astype(BF16)

    pk = _dot(hb, w_ref[:, PC_AK:PC_END])
    ak = head_norm_rope(pk[:, :LANES], kn_ref[...])
    ak_o[0] = ak.astype(BF16)
    avT_o[0] = pk[:, LANES:2 * LANES].T.astype(BF16)

    pab = pk[:, 2 * LANES:]
    n_dh = DN_DIRS * DN_HEADS
    zab = pab + dtb_ref[...]
    softplus = jnp.maximum(zab, 0.0) + jnp.log(1.0 + jnp.exp(-jnp.abs(zab)))
    g = -jnp.exp(alog_ref[...]) * softplus
    beta = 1.0 / (1.0 + jnp.exp(-pab))
    g = jnp.where(lane < n_dh, g, 0.0)
    rr = lax.broadcasted_iota(jnp.int32, (tm, tm), 0)
    cc = lax.broadcasted_iota(jnp.int32, (tm, tm), 1)
    same = (rr // chunk) == (cc // chunk)
    low = jnp.where(jnp.logical_and(same, cc <= rr), 1.0, 0.0).astype(BF16)
    upp = jnp.where(jnp.logical_and(same, cc >= rr), 1.0, 0.0).astype(BF16)
    hi, mid, lo = _split3(g)
    pre = _dot(low, hi) + _dot(low, mid) + _dot(low, lo)
    suf = _dot(upp, hi) + _dot(upp, mid) + _dot(upp, lo)
    gc = jnp.where(lane < DN_HEADS, pre, jnp.where(lane < n_dh, suf, jnp.where(lane < 2 * n_dh, beta, 0.0)))
    gc_o[0] = gc
    gcT_o[0] = gc.T[0:2 * n_dh, :]


def _proj_call(x, mod, npre, w, cw, alog, dtb, qn, kn, cos_t, sin_t, *, q_side, tm, chunk):
    bsz, t_len, d = x.shape
    tm = min(tm, t_len)
    nt = t_len // tm
    f32_out = lambda n: jax.ShapeDtypeStruct((bsz, t_len, n), F32)
    bf16_out = lambda n: jax.ShapeDtypeStruct((bsz, t_len, n), BF16)
    row_spec = lambda n: pl.BlockSpec((1, tm, n), lambda b, t: (b, t, 0))
    out_shape = [bf16_out(DN_W), bf16_out(DN_W), f32_out(LANES),
                 jax.ShapeDtypeStruct((bsz, 2 * DN_DIRS * DN_HEADS, t_len), F32),
                 bf16_out(ATT_KV_W),
                 jax.ShapeDtypeStruct((bsz, ATT_KV_W, t_len), BF16)]
    out_specs = [row_spec(DN_W), row_spec(DN_W), row_spec(LANES),
                 pl.BlockSpec((1, 2 * DN_DIRS * DN_HEADS, tm), lambda b, t: (b, 0, t)),
                 row_spec(ATT_KV_W),
                 pl.BlockSpec((1, ATT_KV_W, tm), lambda b, t: (b, 0, t))]
    if q_side:
        out_shape += [bf16_out(DN_W), f32_out(DN_W), jax.ShapeDtypeStruct((bsz, ATT_Q_W, t_len), BF16)]
        out_specs += [row_spec(DN_W), row_spec(DN_W), pl.BlockSpec((1, ATT_Q_W, tm), lambda b, t: (b, 0, t))]
    vec = lambda n: pl.BlockSpec((1, n), lambda b, t: (0, 0))
    kern = functools.partial(_proj_kernel, tm=tm, chunk=chunk, q_side=q_side)
    return pl.pallas_call(
        kern,
        out_shape=out_shape,
        grid=(bsz, nt),
        in_specs=_halo_specs(tm, t_len, d) + [
            _mod_spec(mod), vec(d), _resident(w.shape), _resident(cw.shape),
            vec(LANES), vec(LANES), vec(LANES), vec(LANES),
            pl.BlockSpec((tm, LANES), lambda b, t: (t, 0)),
            pl.BlockSpec((tm, LANES), lambda b, t: (t, 0)),
        ],
        out_specs=out_specs,
        compiler_params=pltpu.CompilerParams(
            dimension_semantics=("parallel", "arbitrary"), vmem_limit_bytes=VMEM_LIMIT),
        name="proj_q" if q_side else "proj_kv",
    )(x, x, x, mod, npre, w, cw, alog, dtb, qn, kn, cos_t, sin_t)


def _dn_kernel(*refs, chunk, sup, n_ctx, n_lat, ctx_q, hp):
    c = chunk
    nb = sup // c
    n_in_ctx = 5 if ctx_q else 4
    ctx_in = refs[:n_in_ctx]
    lat_in = refs[n_in_ctx:n_in_ctx + 5]
    n_out = 2 if ctx_q else 1
    outs = refs[n_in_ctx + 5:n_in_ctx + 5 + n_out]
    u_s, w_s, kd_s, qg_s, in_s, eg_s = refs[n_in_ctx + 5 + n_out:]
    head0 = pl.program_id(1) * hp
    chains = [(hh, d) for hh in range(hp) for d in range(DN_DIRS)]
    n_dh = DN_DIRS * DN_HEADS

    ri = lax.broadcasted_iota(jnp.int32, (sup, sup), 0)
    ci = lax.broadcasted_iota(jnp.int32, (sup, sup), 1)
    same = (ri // c) == (ci // c)
    incl = [jnp.logical_and(same, ci <= ri), jnp.logical_and(same, ci >= ri)]
    strict = [jnp.logical_and(same, ci < ri), jnp.logical_and(same, ci > ri)]
    rw = lax.broadcasted_iota(jnp.int32, (c, sup), 0)
    cw = lax.broadcasted_iota(jnp.int32, (c, sup), 1)
    cl = cw % c
    lane_blk = cw // c
    eye_w = jnp.where(rw == cl, 1.0, 0.0)
    level_mask = []
    n = 1
    while n < c:
        level_mask.append(jnp.logical_and(rw // (2 * n) == cl // (2 * n), rw // n != cl // n))
        n *= 2
    lane = lax.broadcasted_iota(jnp.int32, (1, LANES), 1)
    sub = lax.broadcasted_iota(jnp.int32, (2 * n_dh, 1), 0)

    def pick_col(x, idx):
        return jnp.sum(jnp.where(lane == idx, x, 0.0), axis=-1, keepdims=True)

    def block_diag(xw):
        return jnp.where(same, jnp.concatenate([xw] * nb, axis=0), 0.0).astype(BF16)

    def prep(slot, src, pos, has_q):
        if has_q:
            k_ref, v_ref, gc_ref, gct_ref, q_ref = src
        else:
            k_ref, v_ref, gc_ref, gct_ref = src
        r0 = [pl.multiple_of(pos[d] * sup, sup) for d in range(DN_DIRS)]
        gcb = [gc_ref[0, pl.ds(r0[d], sup), :] for d in range(DN_DIRS)]
        gct = [gct_ref[0, :, pl.ds(r0[d], sup)] for d in range(DN_DIRS)]
        st = []
        for n, (hh, d) in enumerate(chains):
            hs = slice(hh * DN_DK, (hh + 1) * DN_DK)
            k16 = k_ref[0, pl.ds(r0[d], sup), hs]
            k = k16.astype(F32)
            v = v_ref[0, pl.ds(r0[d], sup), hs].astype(F32)
            q = q_ref[0, pl.ds(r0[d], sup), hs].astype(F32) if has_q else None
            col = d * DN_HEADS + head0 + hh
            gc_col = pick_col(gcb[d], col)
            beta = pick_col(gcb[d], n_dh + col)
            gc_row = jnp.sum(jnp.where(sub == col, gct[d], 0.0), axis=0, keepdims=True)
            decay = jnp.where(incl[d], jnp.exp(jnp.where(incl[d], gc_col - gc_row, 0.0)), 0.0)
            kbeta = k * beta
            if has_q:
                kk = _dot_nt(jnp.concatenate([kbeta, q], axis=0).astype(BF16), k16)
                low = jnp.where(strict[d], kk[:sup] * decay, 0.0)
                intra = kk[sup:] * decay
                for b in range(nb):
                    in_s[slot, n, b * c:(b + 1) * c, :] = intra[b * c:(b + 1) * c, b * c:(b + 1) * c].astype(BF16)
            else:
                low = jnp.where(strict[d], _dot_nt(kbeta.astype(BF16), k16) * decay, 0.0)
            low_w = jnp.where(lane_blk == 0, low[0:c], 0.0)
            for b in range(1, nb):
                low_w = low_w + jnp.where(lane_blk == b, low[b * c:(b + 1) * c], 0.0)
            egc = jnp.exp(gc_col)
            rhs = jnp.concatenate([v * beta, kbeta * egc], axis=1).astype(BF16)
            last = [b * c + (c - 1 if d == 0 else 0) for b in range(nb)]
            g_last = jnp.concatenate([jnp.broadcast_to(gc_col[i:i + 1], (c, 1)) for i in last], axis=0)
            kd_s[slot, n] = (k * jnp.exp(g_last - gc_col)).astype(BF16)
            for b, i in enumerate(last):
                eg_s[slot, n, b] = jnp.broadcast_to(jnp.exp(gc_col[i:i + 1]), (HALO, LANES))
            if has_q:
                qg_s[slot, n] = (q * egc).astype(BF16)
            st.append((low_w, rhs))
            yield

        tw = [eye_w - jnp.where(level_mask[0], s[0], 0.0) for s in st]
        for lm in level_mask[1:]:
            et = [_dot(jnp.where(lm, s[0], 0.0).astype(BF16), block_diag(t)) for s, t in zip(st, tw)]
            yield
            tw = [t - _dot(t.astype(BF16), block_diag(e)) for t, e in zip(tw, et)]
            yield

        for n, (s, t) in enumerate(zip(st, tw)):
            uw = _dot(block_diag(t), s[1])
            u_s[slot, n] = uw[:, :DN_DK]
            w_s[slot, n] = uw[:, DN_DK:].astype(BF16)
            yield

    def scan_step(slot, t, states, rows_out, o_ref, has_q):
        new = []
        rows = [slice(t * c, (t + 1) * c), slice((nb - 1 - t) * c, (nb - t) * c)]
        blk = [t, nb - 1 - t]
        s16 = [s.astype(BF16) for s in states]
        if has_q:
            ws = [_dot(jnp.concatenate([w_s[slot, n, rows[d], :], qg_s[slot, n, rows[d], :]], axis=0), s16[n])
                  for n, (hh, d) in enumerate(chains)]
            v16 = [(u_s[slot, n, rows[d], :] - ws[n][:c]).astype(BF16) for n, (hh, d) in enumerate(chains)]
            for n, (hh, d) in enumerate(chains):
                o = ws[n][c:] + _dot(in_s[slot, n, rows[d], :], v16[n])
                dst = pl.ds(pl.multiple_of(rows_out[d] + blk[d] * c, c), c)
                o_ref[0, dst, hh * DN_DK:(hh + 1) * DN_DK] += o
        else:
            v16 = [(u_s[slot, n, rows[d], :] - _dot(w_s[slot, n, rows[d], :], s16[n])).astype(BF16)
                   for n, (hh, d) in enumerate(chains)]
        for n, (hh, d) in enumerate(chains):
            new.append(states[n] * eg_s[slot, n, blk[d], 0:1, :] + _dot_tn(kd_s[slot, n, rows[d], :], v16[n]))
        return tuple(new)

    def run(prep_gen, slot, states, rows_out, o_ref, has_q):
        holder = [states]
        todo = list(range(nb)) if slot is not None else []
        if prep_gen is not None:
            n_stage = 2 * len(chains) + 2 * (len(level_mask) - 1)
            every = max(n_stage // (nb + 1), 1)
            for i, _ in enumerate(prep_gen):
                if todo and (i + 1) % every == 0:
                    holder[0] = scan_step(slot, todo.pop(0), holder[0], rows_out, o_ref, has_q)
        for t in todo:
            holder[0] = scan_step(slot, t, holder[0], rows_out, o_ref, has_q)
        return holder[0]

    for o_ref in outs:
        o_ref[...] = jnp.zeros_like(o_ref)

    n_all = n_ctx + n_lat
    zero = jnp.zeros((DN_DK, DN_DK), F32)
    states = tuple(zero for _ in chains)

    def ctx_pos(p):
        return (p, n_ctx - 1 - p)

    def lat_pos(p):
        return (p, n_lat - 1 - p)

    def rows_of(pos):
        return [pos[0] * sup, pos[1] * sup]

    o_ctx = outs[1] if ctx_q else None
    run(prep(0, ctx_in, ctx_pos(0), ctx_q), None, states, None, None, False)
    for p in range(n_ctx):
        if p + 1 < n_ctx:
            nxt = prep((p + 1) % 2, ctx_in, ctx_pos(p + 1), ctx_q)
        else:
            nxt = prep((p + 1) % 2, lat_in, lat_pos(0), True)
        states = run(nxt, p % 2, states, rows_of(ctx_pos(p)), o_ctx, ctx_q)

    def lat_body(p, states):
        slot = (n_ctx + p) % 2
        nxt = prep(1 - slot, lat_in, lat_pos(p + 1), True)
        return run(nxt, slot, states, rows_of(lat_pos(p)), outs[0], True)

    states = lax.fori_loop(0, n_lat - 1, lat_body, states)
    run(None, (n_all - 1) % 2, states, rows_of(lat_pos(n_lat - 1)), outs[0], True)


def _dn_call(ctx_in, lat_in, *, chunk, sup, hp, ctx_q):
    bsz, t_lat, _ = lat_in[0].shape
    t_ctx = ctx_in[0].shape[1]
    assert t_ctx % sup == 0 and t_lat % sup == 0 and sup % chunk == 0 and DN_HEADS % hp == 0
    n_ctx, n_lat = t_ctx // sup, t_lat // sup
    n_rows = 2 * DN_DIRS * DN_HEADS
    nb = sup // chunk

    def specs(t_len, has_q):
        head_blk = pl.BlockSpec((1, t_len, hp * DN_DK), lambda b, h: (b, 0, h))
        sp = [head_blk, head_blk,
              pl.BlockSpec((1, t_len, LANES), lambda b, h: (b, 0, 0)),
              pl.BlockSpec((1, n_rows, t_len), lambda b, h: (b, 0, 0))]
        if has_q:
            sp.append(head_blk)
        return sp

    out_shape = [jax.ShapeDtypeStruct((bsz, t_lat, DN_W), F32)]
    out_specs = [pl.BlockSpec((1, t_lat, hp * DN_DK), lambda b, h: (b, 0, h))]
    if ctx_q:
        out_shape.append(jax.ShapeDtypeStruct((bsz, t_ctx, DN_W), F32))
        out_specs.append(pl.BlockSpec((1, t_ctx, hp * DN_DK), lambda b, h: (b, 0, h)))
    nch = hp * DN_DIRS
    scratch = [
        pltpu.VMEM((2, nch, sup, DN_DK), F32),
        pltpu.VMEM((2, nch, sup, DN_DK), BF16),
        pltpu.VMEM((2, nch, sup, DN_DK), BF16),
        pltpu.VMEM((2, nch, sup, DN_DK), BF16),
        pltpu.VMEM((2, nch, sup, chunk), BF16),
        pltpu.VMEM((2, nch, nb, HALO, LANES), F32),
    ]
    kern = functools.partial(_dn_kernel, chunk=chunk, sup=sup, n_ctx=n_ctx, n_lat=n_lat, ctx_q=ctx_q, hp=hp)
    return pl.pallas_call(
        kern,
        out_shape=out_shape,
        grid=(bsz, DN_HEADS // hp),
        in_specs=specs(t_ctx, ctx_q) + specs(t_lat, True),
        out_specs=out_specs,
        scratch_shapes=scratch,
        compiler_params=pltpu.CompilerParams(
            dimension_semantics=("parallel", "arbitrary"), vmem_limit_bytes=VMEM_LIMIT),
        name="dn_q" if ctx_q else "dn_kv",
    )(*ctx_in, *lat_in)


def _attn_kernel(*refs, n_sets):
    q_ref = refs[0]
    kv = refs[1:1 + 2 * n_sets]
    o_ref = refs[1 + 2 * n_sets]
    slab = 64

    def fold(x, op):
        acc = x[0:slab]
        for r in range(slab, x.shape[0], slab):
            acc = op(acc, x[r:r + slab])
        return acc

    def head_scores(h):
        g = h // ATT_GROUPS
        q_t = q_ref[0, h * ATT_HD:(h + 1) * ATT_HD, :]
        zero = jnp.zeros_like(q_t)
        q_pad = jnp.concatenate([q_t, zero] if g == 0 else [zero, q_t], axis=0)
        return [_dot(kv[2 * i][0], q_pad) for i in range(n_sets)]

    def head_out(h, ps, den):
        g = h // ATT_GROUPS
        acc = _dot(kv[1][0, g * ATT_HD:(g + 1) * ATT_HD, :], ps[0])
        for i in range(1, n_sets):
            acc = acc + _dot(kv[2 * i + 1][0, g * ATT_HD:(g + 1) * ATT_HD, :], ps[i])
        return acc / den

    outs = []
    scores = head_scores(0)
    prev = None
    for h in range(ATT_HEADS):
        nxt = head_scores(h + 1) if h + 1 < ATT_HEADS else None
        if prev is not None:
            outs.append(head_out(h - 1, *prev))
        m = fold(scores[0], jnp.maximum)
        for s in scores[1:]:
            m = jnp.maximum(m, fold(s, jnp.maximum))
        m = m.max(axis=0, keepdims=True)
        ps = [jnp.exp2(s - m) for s in scores]
        den = fold(ps[0], jnp.add)
        for p in ps[1:]:
            den = den + fold(p, jnp.add)
        den = den.sum(axis=0, keepdims=True)
        prev = ([p.astype(BF16) for p in ps], den)
        scores = nxt
    outs.append(head_out(ATT_HEADS - 1, *prev))
    o_ref[0] = jnp.concatenate(outs, axis=0).T.astype(BF16)


def _attn_call(q_t, kv_sets, *, tq):
    bsz, _, t_len = q_t.shape
    tq = min(tq, t_len)
    in_specs = [pl.BlockSpec((1, ATT_Q_W, tq), lambda b, t: (b, 0, t))]
    args = [q_t]
    for k, v_t in kv_sets:
        s_len = k.shape[1]
        in_specs += [pl.BlockSpec((1, s_len, ATT_KV_W), lambda b, t: (b, 0, 0)),
                     pl.BlockSpec((1, ATT_KV_W, s_len), lambda b, t: (b, 0, 0))]
        args += [k, v_t]
    return pl.pallas_call(
        functools.partial(_attn_kernel, n_sets=len(kv_sets)),
        out_shape=jax.ShapeDtypeStruct((bsz, t_len, ATT_Q_W), BF16),
        grid=(bsz, t_len // tq),
        in_specs=in_specs,
        out_specs=pl.BlockSpec((1, tq, ATT_Q_W), lambda b, t: (b, t, 0)),
        compiler_params=pltpu.CompilerParams(
            dimension_semantics=("parallel", "arbitrary"), vmem_limit_bytes=VMEM_LIMIT),
        name="attn%d" % len(kv_sets),
    )(*args)


def _mixout_kernel(x_ref, o_ref, z_ref, b_ref, mod_ref, dnn_ref, npost_ref, w_ref, out_ref):
    gate = mod_ref[0, 2:3, :]
    o = o_ref[0]
    z = z_ref[0]
    parts = []
    for hd in range(DN_HEADS):
        seg = o[:, hd * DN_DK:(hd + 1) * DN_DK]
        parts.append(_rms(seg, dnn_ref[...]) * _silu(z[:, hd * DN_DK:(hd + 1) * DN_DK]))
    a = jnp.concatenate(parts, axis=1).astype(BF16)
    y = _dot(a, w_ref[0:DN_W, :]) + _dot(b_ref[0], w_ref[DN_W:, :])
    out_ref[0] = x_ref[0] + gate * _rms(y, npost_ref[...])


def _mixout_call(x, o_dn, z, b_att, mod, dnn, npost, w_out, *, tm):
    bsz, t_len, d = x.shape
    tm = min(tm, t_len)
    row = lambda n: pl.BlockSpec((1, tm, n), lambda b, t: (b, t, 0))
    return pl.pallas_call(
        _mixout_kernel,
        out_shape=jax.ShapeDtypeStruct(x.shape, F32),
        grid=(bsz, t_len // tm),
        in_specs=[row(d), row(DN_W), row(DN_W), row(ATT_Q_W), _mod_spec(mod),
                  pl.BlockSpec((1, DN_DK), lambda b, t: (0, 0)),
                  pl.BlockSpec((1, d), lambda b, t: (0, 0)),
                  _resident(w_out.shape)],
        out_specs=row(d),
        compiler_params=pltpu.CompilerParams(
            dimension_semantics=("parallel", "arbitrary"), vmem_limit_bytes=VMEM_LIMIT),
        name="mixout",
    )(x, o_dn, z, b_att, mod, dnn, npost, w_out)


def _rope_tables(n):
    half = ATT_HD // 2
    r = (jnp.arange(n) // GRID_W).astype(F32)
    col = (jnp.arange(n) % GRID_W).astype(F32)
    inv = ROPE_THETA ** (-jnp.arange(0, half, 2, dtype=F32) / half)
    ang = jnp.concatenate([r[:, None] * inv, col[:, None] * inv], axis=-1)
    cos, sin = jnp.cos(ang), jnp.sin(ang)
    cos_t = jnp.tile(jnp.concatenate([cos, cos], axis=-1), (1, LANES // ATT_HD))
    sin_t = jnp.tile(jnp.concatenate([-sin, sin], axis=-1), (1, LANES // ATT_HD))
    return cos_t, sin_t


def _even_weights(w_in, q_norm, k_norm, a_log, dt_bias):
    d = w_in.shape[0]
    n_ab = 2 * DN_DIRS * DN_HEADS
    off_ab = 2 * DN_W
    off_ak = off_ab + n_ab
    off_av = off_ak + ATT_KV_W
    off_q = off_av + ATT_KV_W
    off_z = off_q + DN_W
    off_aq = off_z + DN_W

    def deinterleave(a, heads):
        lead = a.shape[:-1]
        a = a.reshape(lead + (heads, ATT_HD // 2, 2))
        return jnp.swapaxes(a, -1, -2).reshape(lead + (heads * ATT_HD,))

    w = jnp.concatenate([
        w_in[:, :off_ab], w_in[:, off_q:off_aq],
        deinterleave(w_in[:, off_aq:off_aq + ATT_Q_W], ATT_HEADS),
        deinterleave(w_in[:, off_ak:off_av], ATT_KV_HEADS),
        w_in[:, off_av:off_q], w_in[:, off_ab:off_ak],
        jnp.zeros((d, PC_END - PC_AB - n_ab), w_in.dtype)], axis=1).astype(BF16)
    tile2 = lambda v: jnp.tile(deinterleave(v, 1), LANES // ATT_HD).reshape(1, LANES)
    pad = lambda v: jnp.pad(v.reshape(-1), (0, LANES - v.size)).reshape(1, LANES)
    return w, tile2(q_norm), tile2(k_norm), pad(a_log), pad(dt_bias)


def kernel(x, c, ctx, c_ctx, mod_w, mod_b, norm_mix_pre, norm_mix_post, norm_ffn_pre, norm_ffn_post,
           hyb_w_in, hyb_w_out, dn_conv_w, dn_a_log, dn_dt_bias, dn_out_norm, att_q_norm, att_k_norm,
           sc_w_in, sc_conv_w, sc_w_out, ffn_w_up, ffn_conv_w, ffn_w_down):
    bsz, seq, d = x.shape
    depth = mod_w.shape[0]
    ctx_len = ctx.shape[1]
    fc = 256
    tm_glu, tm_proj, tm_out, tq = 512, 512, 512, 512

    rows = ((bsz + 1 + HALO - 1) // HALO) * HALO
    s_pad = jnp.zeros((rows, d), F32).at[:bsz].set(c).at[bsz].set(c_ctx)
    mod_all = _mod_call(s_pad, mod_w, mod_b).reshape(depth, rows, 6, d)

    cos_l, sin_l = _rope_tables(seq)
    cos_c, sin_c = jnp.ones((ctx_len, LANES), F32), jnp.zeros((ctx_len, LANES), F32)
    vec = lambda v: v.reshape(1, -1)

    for layer in range(depth):
        even = layer % 2 == 0
        ctx_live = any(j % 2 == 0 for j in range(layer + 1, depth))
        mod_l = mod_all[layer, :bsz]
        mod_c = mod_all[layer, bsz:bsz + 1]
        npre, npost = vec(norm_mix_pre[layer]), vec(norm_mix_post[layer])
        if even:
            e = layer // 2
            w, qn, kn, alog, dtb = _even_weights(hyb_w_in[e], att_q_norm[e], att_k_norm[e],
                                                 dn_a_log[e], dn_dt_bias[e])
            cw = dn_conv_w[e]
            w_out = hyb_w_out[e].astype(BF16)
            dnn = vec(dn_out_norm[e])
            proj = functools.partial(_proj_call, npre=npre, w=w, cw=cw, alog=alog, dtb=dtb, qn=qn, kn=kn,
                                     tm=tm_proj, chunk=DN_CHUNK)
            pl_ = proj(x, mod_l, cos_t=cos_l, sin_t=sin_l, q_side=True)
            pc_ = proj(ctx, mod_c, cos_t=cos_c, sin_t=sin_c, q_side=ctx_live)
            k_l, v_l, gc_l, gct_l, ak_l, avt_l, q_l, z_l, aq_l = pl_
            k_c, v_c, gc_c, gct_c, ak_c, avt_c = pc_[:6]
            lat_in = (k_l, v_l, gc_l, gct_l, q_l)
            ctx_in = (k_c, v_c, gc_c, gct_c) + ((pc_[6],) if ctx_live else ())
            o_dn = _dn_call(ctx_in, lat_in, chunk=DN_CHUNK, sup=DN_SUPER, hp=DN_HEADS_PER_STEP, ctx_q=ctx_live)
            b_lat = _attn_call(aq_l, [(ak_c, avt_c), (ak_l, avt_l)], tq=tq)
            x_new = _mixout_call(x, o_dn[0], z_l, b_lat, mod_l, dnn, npost, w_out, tm=tm_out)
            if ctx_live:
                b_ctx = _attn_call(pc_[8], [(ak_c, avt_c)], tq=tq)
                ctx = _mixout_call(ctx, o_dn[1], pc_[7], b_ctx, mod_c, dnn, npost, w_out, tm=tm_out)
            x = x_new
        else:
            o = layer // 2
            sc = functools.partial(_glu_call, npre=npre, npost=npost, win=sc_w_in[o].astype(BF16), cw=sc_conv_w[o],
                                   wout=sc_w_out[o].astype(BF16), mode="sc", rows=(0, 1, 2), tm=tm_glu, fc=fc)
            x = sc(x, mod_l)
            if ctx_live:
                ctx = sc(ctx, mod_c)
        ffn = functools.partial(_glu_call, npre=vec(norm_ffn_pre[layer]), npost=vec(norm_ffn_post[layer]),
                                win=ffn_w_up[layer].astype(BF16), cw=ffn_conv_w[layer],
                                wout=ffn_w_down[layer].astype(BF16), mode="ffn", rows=(3, 4, 5), tm=tm_glu, fc=fc)
        x = ffn(x, mod_l)
        if ctx_live:
            ctx = ffn(ctx, mod_c)
    return x
```

```python
import functools

import jax
import jax.numpy as jnp
from jax import lax
from jax.experimental import pallas as pl
from jax.experimental.pallas import tpu as pltpu

F32 = jnp.float32
BF16 = jnp.bfloat16
EPS = 1e-6
HALO = 8
LANES = 128
LOG2E = 1.4426950408889634

D_MODEL = 1024
GRID_W = 64
ROPE_THETA = 10000.0
DN_HEADS = 4
DN_DK = 128
DN_DIRS = 2
DN_CHUNK = 64
DN_SUPER = 256
DN_HEADS_PER_STEP = 4
ATT_HEADS = 8
ATT_KV_HEADS = 2
ATT_HD = 64
ATT_GROUPS = ATT_HEADS // ATT_KV_HEADS
D_FF = 2816
DN_W = DN_HEADS * DN_DK
ATT_Q_W = ATT_HEADS * ATT_HD
ATT_KV_W = ATT_KV_HEADS * ATT_HD

PC_K, PC_V, PC_Q, PC_Z, PC_AQ, PC_AK, PC_AV, PC_AB, PC_END = (
    0, 512, 1024, 1536, 2048, 2560, 2688, 2816, 2944)

VMEM_LIMIT = 56 * 1024 * 1024


def _resident(shape):
    nd = len(shape)
    return pl.BlockSpec(shape, lambda *_: (0,) * nd, pipeline_mode=pl.Buffered(1))


def _rms(x, w):
    return x * lax.rsqrt(jnp.mean(x * x, axis=-1, keepdims=True) + EPS) * w


def _silu(x):
    return x * (1.0 / (1.0 + jnp.exp(-x)))


def _dot(a, b):
    return jnp.dot(a, b, preferred_element_type=F32)


def _dot_nt(a, b):
    return lax.dot_general(a, b, (((1,), (1,)), ((), ())), preferred_element_type=F32)


def _dot_tn(a, b):
    return lax.dot_general(a, b, (((0,), (0,)), ((), ())), preferred_element_type=F32)


def _split3(x):
    hi = x.astype(BF16)
    r = x - hi.astype(F32)
    mid = r.astype(BF16)
    lo = (r - mid.astype(F32)).astype(BF16)
    return hi, mid, lo


def _valid_rows(tm):
    t = pl.program_id(1)
    nt = pl.num_programs(1)
    r = lax.broadcasted_iota(jnp.int32, (tm + 2 * HALO, 1), 0)
    lo_ok = jnp.logical_or(r >= HALO, t > 0)
    hi_ok = jnp.logical_or(r < tm + HALO, t < nt - 1)
    return jnp.logical_and(lo_ok, hi_ok)


def _conv3(cin, cw, tm):
    return (cw[0:1] * cin[HALO - 1:HALO - 1 + tm] + cw[1:2] * cin[HALO:HALO + tm]
            + cw[2:3] * cin[HALO + 1:HALO + 1 + tm])


def _halo_specs(tm, t_len, d):
    tb = tm // HALO
    nhb = t_len // HALO
    return [
        pl.BlockSpec((1, tm, d), lambda b, t: (b, t, 0)),
        pl.BlockSpec((1, HALO, d), lambda b, t: (b, jnp.maximum(t * tb - 1, 0), 0)),
        pl.BlockSpec((1, HALO, d), lambda b, t: (b, jnp.minimum((t + 1) * tb, nhb - 1), 0)),
    ]


def _mod_spec(mod):
    if mod.shape[0] > 1:
        return pl.BlockSpec((1, 6, mod.shape[2]), lambda b, t: (b, 0, 0))
    return pl.BlockSpec((1, 6, mod.shape[2]), lambda b, t: (0, 0, 0))


def _mod_kernel(s_ref, w_ref, b_ref, o_ref):
    s = _silu(s_ref[...])
    o_ref[0] = jnp.dot(s, w_ref[0], preferred_element_type=F32,
                       precision=lax.Precision.HIGHEST) + b_ref[0]


def _mod_call(s_pad, mod_w, mod_b):
    depth, d, n6 = mod_w.shape
    rows = s_pad.shape[0]
    nb = n6 // d
    return pl.pallas_call(
        _mod_kernel,
        out_shape=jax.ShapeDtypeStruct((depth, rows, n6), F32),
        grid=(depth, nb),
        in_specs=[
            pl.BlockSpec((rows, d), lambda l, j: (0, 0)),
            pl.BlockSpec((1, d, d), lambda l, j: (l, 0, j)),
            pl.BlockSpec((1, 1, d), lambda l, j: (l, 0, j)),
        ],
        out_specs=pl.BlockSpec((1, rows, d), lambda l, j: (l, 0, j)),
        compiler_params=pltpu.CompilerParams(
            dimension_semantics=("arbitrary", "arbitrary"), vmem_limit_bytes=VMEM_LIMIT),
        name="mod",
    )(s_pad, mod_w, mod_b.reshape(depth, 1, n6))


def _glu_kernel(xm_ref, xp_ref, xn_ref, mod_ref, npre_ref, npost_ref, win_ref, cw_ref, wout_ref,
                o_ref, g_ref, *, mode, tm, nf, fc, rows):
    sh_i, sc_i, gt_i = rows
    x_main = xm_ref[0]
    x_ext = jnp.concatenate([xp_ref[0], x_main, xn_ref[0]], axis=0)
    shift = mod_ref[0, sh_i:sh_i + 1, :]
    scale = mod_ref[0, sc_i:sc_i + 1, :]
    gate = mod_ref[0, gt_i:gt_i + 1, :]
    hb = (_rms(x_ext, npre_ref[...]) * (1.0 + scale) + shift).astype(BF16)
    valid = _valid_rows(tm)
    nbr = 2 if mode == "ffn" else 3
    f_all = nf * fc

    def up(j):
        return [_dot(hb, win_ref[:, b * f_all + j * fc:b * f_all + (j + 1) * fc]) for b in range(nbr)]

    def unit(j, p):
        cw = cw_ref[:, j * fc:(j + 1) * fc]
        if mode == "ffn":
            conv = _conv3(jnp.where(valid, p[0], 0.0), cw, tm)
            g = _silu(conv) * p[1][HALO:HALO + tm]
        else:
            conv = _conv3(jnp.where(valid, p[1] * p[2], 0.0), cw, tm)
            g = p[0][HALO:HALO + tm] * conv
        g_ref[:, j * fc:(j + 1) * fc] = g.astype(BF16)

    p = up(0)
    for j in range(nf):
        p_next = up(j + 1) if j + 1 < nf else None
        unit(j, p)
        p = p_next
    y = _dot(g_ref[...], wout_ref[...])
    o_ref[0] = x_main + gate * _rms(y, npost_ref[...])


def _glu_call(x, mod, npre, npost, win, cw, wout, *, mode, rows, tm, fc):
    bsz, t_len, d = x.shape
    f_all = wout.shape[0]
    assert f_all % fc == 0 and fc % LANES == 0
    nf = f_all // fc
    tm = min(tm, t_len)
    kern = functools.partial(_glu_kernel, mode=mode, tm=tm, nf=nf, fc=fc, rows=rows)
    return pl.pallas_call(
        kern,
        out_shape=jax.ShapeDtypeStruct(x.shape, F32),
        grid=(bsz, t_len // tm),
        in_specs=_halo_specs(tm, t_len, d) + [
            _mod_spec(mod),
            pl.BlockSpec((1, d), lambda b, t: (0, 0)),
            pl.BlockSpec((1, d), lambda b, t: (0, 0)),
            _resident(win.shape), _resident(cw.shape), _resident(wout.shape),
        ],
        out_specs=pl.BlockSpec((1, tm, d), lambda b, t: (b, t, 0)),
        scratch_shapes=[pltpu.VMEM((tm, f_all), BF16)],
        compiler_params=pltpu.CompilerParams(
            dimension_semantics=("parallel", "arbitrary"), vmem_limit_bytes=VMEM_LIMIT),
        name="glu_" + mode,
    )(x, x, x, mod, npre, npost, win, cw, wout)


def _proj_kernel(*refs, tm, chunk, q_side):
    (xm_ref, xp_ref, xn_ref, mod_ref, npre_ref, w_ref, cw_ref, alog_ref, dtb_ref,
     qn_ref, kn_ref, cos_ref, sin_ref) = refs[:13]
    outs = refs[13:]
    k_o, v_o, gc_o, gcT_o, ak_o, avT_o = outs[:6]
    if q_side:
        q_o, z_o, aq_o = outs[6:9]

    x_main = xm_ref[0]
    x_ext = jnp.concatenate([xp_ref[0], x_main, xn_ref[0]], axis=0)
    shift = mod_ref[0, 0:1, :]
    scale = mod_ref[0, 1:2, :]
    npre = npre_ref[...]
    hb_ext = (_rms(x_ext, npre) * (1.0 + scale) + shift).astype(BF16)
    hb = (_rms(x_main, npre) * (1.0 + scale) + shift).astype(BF16)
    valid = _valid_rows(tm)

    groups = [(PC_K, k_o, True, 1.0), (PC_V, v_o, False, 1.0)]
    if q_side:
        groups.append((PC_Q, q_o, True, DN_DK ** -0.5))
    for c0, o_ref, unit, mult in groups:
        p = jnp.where(valid, _dot(hb_ext, w_ref[:, c0:c0 + DN_W]), 0.0)
        a = _silu(_conv3(p, cw_ref[:, c0:c0 + DN_W], tm))
        if unit:
            for hd in range(DN_HEADS):
                seg = a[:, hd * DN_DK:(hd + 1) * DN_DK]
                inv = lax.rsqrt(jnp.sum(seg * seg, axis=-1, keepdims=True) + EPS)
                o_ref[0, :, hd * DN_DK:(hd + 1) * DN_DK] = (seg * (inv * mult)).astype(BF16)
        else:
            o_ref[0] = a.astype(BF16)

    lane = lax.broadcasted_iota(jnp.int32, (1, LANES), 1)
    half = (lane % ATT_HD) < (ATT_HD // 2)
    cos_t = cos_ref[...]
    sin_t = sin_ref[...]
    r_i = lax.broadcasted_iota(jnp.int32, (LANES, LANES), 0)
    c_i = lax.broadcasted_iota(jnp.int32, (LANES, LANES), 1)
    head_mean = jnp.where((r_i // ATT_HD) == (c_i // ATT_HD), 1.0 / ATT_HD, 0.0).astype(BF16)

    def head_norm_rope(xa, wn):
        hi, mid, lo = _split3(xa * xa)
        ms = _dot(hi, head_mean) + _dot(mid, head_mean) + _dot(lo, head_mean)
        xa = xa * lax.rsqrt(ms + EPS) * wn
        swapped = jnp.where(half, pltpu.roll(xa, LANES - ATT_HD // 2, 1), pltpu.roll(xa, ATT_HD // 2, 1))
        return xa * cos_t + swapped * sin_t

    if q_side:
        z_o[0] = _dot(hb, w_ref[:, PC_Z:PC_AQ])
        pq = _dot(hb, w_ref[:, PC_AQ:PC_AK])
        for gq in range(ATT_Q_W // LANES):
            xa = head_norm_rope(pq[:, gq * LANES:(gq + 1) * LANES], qn_ref[...])
            aq_o[0, gq * LANES:(gq + 1) * LANES, :] = (xa * (ATT_HD ** -0.5 * LOG2E)).T.astype(BF16)

    pk = _dot(hb, w_ref[:, PC_AK:PC_END])
    ak = head_norm_rope(pk[:, :LANES], kn_ref[...])
    ak_o[0] = ak.astype(BF16)
    avT_o[0] = pk[:, LANES:2 * LANES].T.astype(BF16)

    pab = pk[:, 2 * LANES:]
    n_dh = DN_DIRS * DN_HEADS
    zab = pab + dtb_ref[...]
    softplus = jnp.maximum(zab, 0.0) + jnp.log(1.0 + jnp.exp(-jnp.abs(zab)))
    g = -jnp.exp(alog_ref[...]) * softplus
    beta = 1.0 / (1.0 + jnp.exp(-pab))
    g = jnp.where(lane < n_dh, g, 0.0)
    rr = lax.broadcasted_iota(jnp.int32, (tm, tm), 0)
    cc = lax.broadcasted_iota(jnp.int32, (tm, tm), 1)
    same = (rr // chunk) == (cc // chunk)
    low = jnp.where(jnp.logical_and(same, cc <= rr), 1.0, 0.0).astype(BF16)
    upp = jnp.where(jnp.logical_and(same, cc >= rr), 1.0, 0.0).astype(BF16)
    hi, mid, lo = _split3(g)
    pre = _dot(low, hi) + _dot(low, mid) + _dot(low, lo)
    suf = _dot(upp, hi) + _dot(upp, mid) + _dot(upp, lo)
    gc = jnp.where(lane < DN_HEADS, pre, jnp.where(lane < n_dh, suf, jnp.where(lane < 2 * n_dh, beta, 0.0)))
    gc_o[0] = gc
    gcT_o[0] = gc.T[0:2 * n_dh, :]


def _proj_call(x, mod, npre, w, cw, alog, dtb, qn, kn, cos_t, sin_t, *, q_side, tm, chunk):
    bsz, t_len, d = x.shape
    tm = min(tm, t_len)
    nt = t_len // tm
    f32_out = lambda n: jax.ShapeDtypeStruct((bsz, t_len, n), F32)
    bf16_out = lambda n: jax.ShapeDtypeStruct((bsz, t_len, n), BF16)
    row_spec = lambda n: pl.BlockSpec((1, tm, n), lambda b, t: (b, t, 0))
    out_shape = [bf16_out(DN_W), bf16_out(DN_W), f32_out(LANES),
                 jax.ShapeDtypeStruct((bsz, 2 * DN_DIRS * DN_HEADS, t_len), F32),
                 bf16_out(ATT_KV_W),
                 jax.ShapeDtypeStruct((bsz, ATT_KV_W, t_len), BF16)]
    out_specs = [row_spec(DN_W), row_spec(DN_W), row_spec(LANES),
                 pl.BlockSpec((1, 2 * DN_DIRS * DN_HEADS, tm), lambda b, t: (b, 0, t)),
                 row_spec(ATT_KV_W),
                 pl.BlockSpec((1, ATT_KV_W, tm), lambda b, t: (b, 0, t))]
    if q_side:
        out_shape += [bf16_out(DN_W), f32_out(DN_W), jax.ShapeDtypeStruct((bsz, ATT_Q_W, t_len), BF16)]
        out_specs += [row_spec(DN_W), row_spec(DN_W), pl.BlockSpec((1, ATT_Q_W, tm), lambda b, t: (b, 0, t))]
    vec = lambda n: pl.BlockSpec((1, n), lambda b, t: (0, 0))
    kern = functools.partial(_proj_kernel, tm=tm, chunk=chunk, q_side=q_side)
    return pl.pallas_call(
        kern,
        out_shape=out_shape,
        grid=(bsz, nt),
        in_specs=_halo_specs(tm, t_len, d) + [
            _mod_spec(mod), vec(d), _resident(w.shape), _resident(cw.shape),
            vec(LANES), vec(LANES), vec(LANES), vec(LANES),
            pl.BlockSpec((tm, LANES), lambda b, t: (t, 0)),
            pl.BlockSpec((tm, LANES), lambda b, t: (t, 0)),
        ],
        out_specs=out_specs,
        compiler_params=pltpu.CompilerParams(
            dimension_semantics=("parallel", "arbitrary"), vmem_limit_bytes=VMEM_LIMIT),
        name="proj_q" if q_side else "proj_kv",
    )(x, x, x, mod, npre, w, cw, alog, dtb, qn, kn, cos_t, sin_t)


def _dn_kernel(*refs, chunk, sup, n_ctx, n_lat, ctx_q, hp):
    c = chunk
    nb = sup // c
    n_in_ctx = 5 if ctx_q else 4
    ctx_in = refs[:n_in_ctx]
    lat_in = refs[n_in_ctx:n_in_ctx + 5]
    n_out = 2 if ctx_q else 1
    outs = refs[n_in_ctx + 5:n_in_ctx + 5 + n_out]
    u_s, w_s, kd_s, qg_s, in_s, eg_s = refs[n_in_ctx + 5 + n_out:]
    head0 = pl.program_id(1) * hp
    chains = [(hh, d) for hh in range(hp) for d in range(DN_DIRS)]
    n_dh = DN_DIRS * DN_HEADS

    ri = lax.broadcasted_iota(jnp.int32, (sup, sup), 0)
    ci = lax.broadcasted_iota(jnp.int32, (sup, sup), 1)
    same = (ri // c) == (ci // c)
    incl = [jnp.logical_and(same, ci <= ri), jnp.logical_and(same, ci >= ri)]
    strict = [jnp.logical_and(same, ci < ri), jnp.logical_and(same, ci > ri)]
    rw = lax.broadcasted_iota(jnp.int32, (c, sup), 0)
    cw = lax.broadcasted_iota(jnp.int32, (c, sup), 1)
    cl = cw % c
    lane_blk = cw // c
    eye_w = jnp.where(rw == cl, 1.0, 0.0)
    level_mask = []
    n = 1
    while n < c:
        level_mask.append(jnp.logical_and(rw // (2 * n) == cl // (2 * n), rw // n != cl // n))
        n *= 2
    lane = lax.broadcasted_iota(jnp.int32, (1, LANES), 1)
    sub = lax.broadcasted_iota(jnp.int32, (2 * n_dh, 1), 0)

    def pick_col(x, idx):
        return jnp.sum(jnp.where(lane == idx, x, 0.0), axis=-1, keepdims=True)

    def block_diag(xw):
        return jnp.where(same, jnp.concatenate([xw] * nb, axis=0), 0.0).astype(BF16)

    def prep(slot, src, pos, has_q):
        if has_q:
            k_ref, v_ref, gc_ref, gct_ref, q_ref = src
        else:
            k_ref, v_ref, gc_ref, gct_ref = src
        r0 = [pl.multiple_of(pos[d] * sup, sup) for d in range(DN_DIRS)]
        gcb = [gc_ref[0, pl.ds(r0[d], sup), :] for d in range(DN_DIRS)]
        gct = [gct_ref[0, :, pl.ds(r0[d], sup)] for d in range(DN_DIRS)]
        st = []
        for n, (hh, d) in enumerate(chains):
            hs = slice(hh * DN_DK, (hh + 1) * DN_DK)
            k16 = k_ref[0, pl.ds(r0[d], sup), hs]
            k = k16.astype(F32)
            v = v_ref[0, pl.ds(r0[d], sup), hs].astype(F32)
            q = q_ref[0, pl.ds(r0[d], sup), hs].astype(F32) if has_q else None
            col = d * DN_HEADS + head0 + hh
            gc_col = pick_col(gcb[d], col)
            beta = pick_col(gcb[d], n_dh + col)
            gc_row = jnp.sum(jnp.where(sub == col, gct[d], 0.0), axis=0, keepdims=True)
            decay = jnp.where(incl[d], jnp.exp(jnp.where(incl[d], gc_col - gc_row, 0.0)), 0.0)
            kbeta = k * beta
            if has_q:
                kk = _dot_nt(jnp.concatenate([kbeta, q], axis=0).astype(BF16), k16)
                low = jnp.where(strict[d], kk[:sup] * decay, 0.0)
                intra = kk[sup:] * decay
                for b in range(nb):
                    in_s[slot, n, b * c:(b + 1) * c, :] = intra[b * c:(b + 1) * c, b * c:(b + 1) * c].astype(BF16)
            else:
                low = jnp.where(strict[d], _dot_nt(kbeta.astype(BF16), k16) * decay, 0.0)
            low_w = jnp.where(lane_blk == 0, low[0:c], 0.0)
            for b in range(1, nb):
                low_w = low_w + jnp.where(lane_blk == b, low[b * c:(b + 1) * c], 0.0)
            egc = jnp.exp(gc_col)
            rhs = jnp.concatenate([v * beta, kbeta * egc], axis=1).astype(BF16)
            last = [b * c + (c - 1 if d == 0 else 0) for b in range(nb)]
            g_last = jnp.concatenate([jnp.broadcast_to(gc_col[i:i + 1], (c, 1)) for i in last], axis=0)
            kd_s[slot, n] = (k * jnp.exp(g_last - gc_col)).astype(BF16)
            for b, i in enumerate(last):
                eg_s[slot, n, b] = jnp.broadcast_to(jnp.exp(gc_col[i:i + 1]), (HALO, LANES))
            if has_q:
                qg_s[slot, n] = (q * egc).astype(BF16)
            st.append((low_w, rhs))
            yield

        tw = [eye_w - jnp.where(level_mask[0], s[0], 0.0) for s in st]
        for lm in level_mask[1:]:
            et = [_dot(jnp.where(lm, s[0], 0.0).astype(BF16), block_diag(t)) for s, t in zip(st, tw)]
            yield
            tw = [t - _dot(t.astype(BF16), block_diag(e)) for t, e in zip(tw, et)]
            yield

        for n, (s, t) in enumerate(zip(st, tw)):
            uw = _dot(block_diag(t), s[1])
            u_s[slot, n] = uw[:, :DN_DK]
            w_s[slot, n] = uw[:, DN_DK:].astype(BF16)
            yield

    def scan_step(slot, t, states, rows_out, o_ref, has_q):
        new = []
        rows = [slice(t * c, (t + 1) * c), slice((nb - 1 - t) * c, (nb - t) * c)]
        blk = [t, nb - 1 - t]
        s16 = [s.astype(BF16) for s in states]
        if has_q:
            ws = [_dot(jnp.concatenate([w_s[slot, n, rows[d], :], qg_s[slot, n, rows[d], :]], axis=0), s16[n])
                  for n, (hh, d) in enumerate(chains)]
            v16 = [(u_s[slot, n, rows[d], :] - ws[n][:c]).astype(BF16) for n, (hh, d) in enumerate(chains)]
            for n, (hh, d) in enumerate(chains):
                o = ws[n][c:] + _dot(in_s[slot, n, rows[d], :], v16[n])
                dst = pl.ds(pl.multiple_of(rows_out[d] + blk[d] * c, c), c)
                o_ref[0, dst, hh * DN_DK:(hh + 1) * DN_DK] += o
        else:
            v16 = [(u_s[slot, n, rows[d], :] - _dot(w_s[slot, n, rows[d], :], s16[n])).astype(BF16)
                   for n, (hh, d) in enumerate(chains)]
        for n, (hh, d) in enumerate(chains):
            new.append(states[n] * eg_s[slot, n, blk[d], 0:1, :] + _dot_tn(kd_s[slot, n, rows[d], :], v16[n]))
        return tuple(new)

    def run(prep_gen, slot, states, rows_out, o_ref, has_q):
        holder = [states]
        todo = list(range(nb)) if slot is not None else []
        if prep_gen is not None:
            n_stage = 2 * len(chains) + 2 * (len(level_mask) - 1)
            every = max(n_stage // (nb + 1), 1)
            for i, _ in enumerate(prep_gen):
                if todo and (i + 1) % every == 0:
                    holder[0] = scan_step(slot, todo.pop(0), holder[0], rows_out, o_ref, has_q)
        for t in todo:
            holder[0] = scan_step(slot, t, holder[0], rows_out, o_ref, has_q)
        return holder[0]

    for o_ref in outs:
        o_ref[...] = jnp.zeros_like(o_ref)

    n_all = n_ctx + n_lat
    zero = jnp.zeros((DN_DK, DN_DK), F32)
    states = tuple(zero for _ in chains)

    def ctx_pos(p):
        return (p, n_ctx - 1 - p)

    def lat_pos(p):
        return (p, n_lat - 1 - p)

    def rows_of(pos):
        return [pos[0] * sup, pos[1] * sup]

    o_ctx = outs[1] if ctx_q else None
    run(prep(0, ctx_in, ctx_pos(0), ctx_q), None, states, None, None, False)
    for p in range(n_ctx):
        if p + 1 < n_ctx:
            nxt = prep((p + 1) % 2, ctx_in, ctx_pos(p + 1), ctx_q)
        else:
            nxt = prep((p + 1) % 2, lat_in, lat_pos(0), True)
        states = run(nxt, p % 2, states, rows_of(ctx_pos(p)), o_ctx, ctx_q)

    def lat_body(p, states):
        slot = (n_ctx + p) % 2
        nxt = prep(1 - slot, lat_in, lat_pos(p + 1), True)
        return run(nxt, slot, states, rows_of(lat_pos(p)), outs[0], True)

    states = lax.fori_loop(0, n_lat - 1, lat_body, states)
    run(None, (n_all - 1) % 2, states, rows_of(lat_pos(n_lat - 1)), outs[0], True)


def _dn_call(ctx_in, lat_in, *, chunk, sup, hp, ctx_q):
    bsz, t_lat, _ = lat_in[0].shape
    t_ctx = ctx_in[0].shape[1]
    assert t_ctx % sup == 0 and t_lat % sup == 0 and sup % chunk == 0 and DN_HEADS % hp == 0
    n_ctx, n_lat = t_ctx // sup, t_lat // sup
    n_rows = 2 * DN_DIRS * DN_HEADS
    nb = sup // chunk

    def specs(t_len, has_q):
        head_blk = pl.BlockSpec((1, t_len, hp * DN_DK), lambda b, h: (b, 0, h))
        sp = [head_blk, head_blk,
              pl.BlockSpec((1, t_len, LANES), lambda b, h: (b, 0, 0)),
              pl.BlockSpec((1, n_rows, t_len), lambda b, h: (b, 0, 0))]
        if has_q:
            sp.append(head_blk)
        return sp

    out_shape = [jax.ShapeDtypeStruct((bsz, t_lat, DN_W), F32)]
    out_specs = [pl.BlockSpec((1, t_lat, hp * DN_DK), lambda b, h: (b, 0, h))]
    if ctx_q:
        out_shape.append(jax.ShapeDtypeStruct((bsz, t_ctx, DN_W), F32))
        out_specs.append(pl.BlockSpec((1, t_ctx, hp * DN_DK), lambda b, h: (b, 0, h)))
    nch = hp * DN_DIRS
    scratch = [
        pltpu.VMEM((2, nch, sup, DN_DK), F32),
        pltpu.VMEM((2, nch, sup, DN_DK), BF16),
        pltpu.VMEM((2, nch, sup, DN_DK), BF16),
        pltpu.VMEM((2, nch, sup, DN_DK), BF16),
        pltpu.VMEM((2, nch, sup, chunk), BF16),
        pltpu.VMEM((2, nch, nb, HALO, LANES), F32),
    ]
    kern = functools.partial(_dn_kernel, chunk=chunk, sup=sup, n_ctx=n_ctx, n_lat=n_lat, ctx_q=ctx_q, hp=hp)
    return pl.pallas_call(
        kern,
        out_shape=out_shape,
        grid=(bsz, DN_HEADS // hp),
        in_specs=specs(t_ctx, ctx_q) + specs(t_lat, True),
        out_specs=out_specs,
        scratch_shapes=scratch,
        compiler_params=pltpu.CompilerParams(
            dimension_semantics=("parallel", "arbitrary"), vmem_limit_bytes=VMEM_LIMIT),
        name="dn_q" if ctx_q else "dn_kv",
    )(*ctx_in, *lat_in)


def _attn_kernel(*refs, n_sets):
    q_ref = refs[0]
    kv = refs[1:1 + 2 * n_sets]
    o_ref = refs[1 + 2 * n_sets]
    slab = 64

    def fold(x, op):
        acc = x[0:slab]
        for r in range(slab, x.shape[0], slab):
            acc = op(acc, x[r:r + slab])
        return acc

    def head_scores(h):
        g = h // ATT_GROUPS
        q_t = q_ref[0, h * ATT_HD:(h + 1) * ATT_HD, :]
        zero = jnp.zeros_like(q_t)
        q_pad = jnp.concatenate([q_t, zero] if g == 0 else [zero, q_t], axis=0)
        return [_dot(kv[2 * i][0], q_pad) for i in range(n_sets)]

    def head_out(h, ps, den):
        g = h // ATT_GROUPS
        acc = _dot(kv[1][0, g * ATT_HD:(g + 1) * ATT_HD, :], ps[0])
        for i in range(1, n_sets):
            acc = acc + _dot(kv[2 * i + 1][0, g * ATT_HD:(g + 1) * ATT_HD, :], ps[i])
        return acc / den

    outs = []
    scores = head_scores(0)
    prev = None
    for h in range(ATT_HEADS):
        nxt = head_scores(h + 1) if h + 1 < ATT_HEADS else None
        if prev is not None:
            outs.append(head_out(h - 1, *prev))
        m = fold(scores[0], jnp.maximum)
        for s in scores[1:]:
            m = jnp.maximum(m, fold(s, jnp.maximum))
        m = m.max(axis=0, keepdims=True)
        ps = [jnp.exp2(s - m) for s in scores]
        den = fold(ps[0], jnp.add)
        for p in ps[1:]:
            den = den + fold(p, jnp.add)
        den = den.sum(axis=0, keepdims=True)
        prev = ([p.astype(BF16) for p in ps], den)
        scores = nxt
    outs.append(head_out(ATT_HEADS - 1, *prev))
    o_ref[0] = jnp.concatenate(outs, axis=0).T.astype(BF16)


def _attn_call(q_t, kv_sets, *, tq):
    bsz, _, t_len = q_t.shape
    tq = min(tq, t_len)
    in_specs = [pl.BlockSpec((1, ATT_Q_W, tq), lambda b, t: (b, 0, t))]
    args = [q_t]
    for k, v_t in kv_sets:
        s_len = k.shape[1]
        in_specs += [pl.BlockSpec((1, s_len, ATT_KV_W), lambda b, t: (b, 0, 0)),
                     pl.BlockSpec((1, ATT_KV_W, s_len), lambda b, t: (b, 0, 0))]
        args += [k, v_t]
    return pl.pallas_call(
        functools.partial(_attn_kernel, n_sets=len(kv_sets)),
        out_shape=jax.ShapeDtypeStruct((bsz, t_len, ATT_Q_W), BF16),
        grid=(bsz, t_len // tq),
        in_specs=in_specs,
        out_specs=pl.BlockSpec((1, tq, ATT_Q_W), lambda b, t: (b, t, 0)),
        compiler_params=pltpu.CompilerParams(
            dimension_semantics=("parallel", "arbitrary"), vmem_limit_bytes=VMEM_LIMIT),
        name="attn%d" % len(kv_sets),
    )(*args)


def _mixout_kernel(x_ref, o_ref, z_ref, b_ref, mod_ref, dnn_ref, npost_ref, w_ref, out_ref):
    gate = mod_ref[0, 2:3, :]
    o = o_ref[0]
    z = z_ref[0]
    parts = []
    for hd in range(DN_HEADS):
        seg = o[:, hd * DN_DK:(hd + 1) * DN_DK]
        parts.append(_rms(seg, dnn_ref[...]) * _silu(z[:, hd * DN_DK:(hd + 1) * DN_DK]))
    a = jnp.concatenate(parts, axis=1).astype(BF16)
    y = _dot(a, w_ref[0:DN_W, :]) + _dot(b_ref[0], w_ref[DN_W:, :])
    out_ref[0] = x_ref[0] + gate * _rms(y, npost_ref[...])


def _mixout_call(x, o_dn, z, b_att, mod, dnn, npost, w_out, *, tm):
    bsz, t_len, d = x.shape
    tm = min(tm, t_len)
    row = lambda n: pl.BlockSpec((1, tm, n), lambda b, t: (b, t, 0))
    return pl.pallas_call(
        _mixout_kernel,
        out_shape=jax.ShapeDtypeStruct(x.shape, F32),
        grid=(bsz, t_len // tm),
        in_specs=[row(d), row(DN_W), row(DN_W), row(ATT_Q_W), _mod_spec(mod),
                  pl.BlockSpec((1, DN_DK), lambda b, t: (0, 0)),
                  pl.BlockSpec((1, d), lambda b, t: (0, 0)),
                  _resident(w_out.shape)],
        out_specs=row(d),
        compiler_params=pltpu.CompilerParams(
            dimension_semantics=("parallel", "arbitrary"), vmem_limit_bytes=VMEM_LIMIT),
        name="mixout",
    )(x, o_dn, z, b_att, mod, dnn, npost, w_out)


def _rope_tables(n):
    half = ATT_HD // 2
    r = (jnp.arange(n) // GRID_W).astype(F32)
    col = (jnp.arange(n) % GRID_W).astype(F32)
    inv = ROPE_THETA ** (-jnp.arange(0, half, 2, dtype=F32) / half)
    ang = jnp.concatenate([r[:, None] * inv, col[:, None] * inv], axis=-1)
    cos, sin = jnp.cos(ang), jnp.sin(ang)
    cos_t = jnp.tile(jnp.concatenate([cos, cos], axis=-1), (1, LANES // ATT_HD))
    sin_t = jnp.tile(jnp.concatenate([-sin, sin], axis=-1), (1, LANES // ATT_HD))
    return cos_t, sin_t


def _even_weights(w_in, q_norm, k_norm, a_log, dt_bias):
    d = w_in.shape[0]
    n_ab = 2 * DN_DIRS * DN_HEADS
    off_ab = 2 * DN_W
    off_ak = off_ab + n_ab
    off_av = off_ak + ATT_KV_W
    off_q = off_av + ATT_KV_W
    off_z = off_q + DN_W
    off_aq = off_z + DN_W

    def deinterleave(a, heads):
        lead = a.shape[:-1]
        a = a.reshape(lead + (heads, ATT_HD // 2, 2))
        return jnp.swapaxes(a, -1, -2).reshape(lead + (heads * ATT_HD,))

    w = jnp.concatenate([
        w_in[:, :off_ab], w_in[:, off_q:off_aq],
        deinterleave(w_in[:, off_aq:off_aq + ATT_Q_W], ATT_HEADS),
        deinterleave(w_in[:, off_ak:off_av], ATT_KV_HEADS),
        w_in[:, off_av:off_q], w_in[:, off_ab:off_ak],
        jnp.zeros((d, PC_END - PC_AB - n_ab), w_in.dtype)], axis=1).astype(BF16)
    tile2 = lambda v: jnp.tile(deinterleave(v, 1), LANES // ATT_HD).reshape(1, LANES)
    pad = lambda v: jnp.pad(v.reshape(-1), (0, LANES - v.size)).reshape(1, LANES)
    return w, tile2(q_norm), tile2(k_norm), pad(a_log), pad(dt_bias)


def kernel(x, c, ctx, c_ctx, mod_w, mod_b, norm_mix_pre, norm_mix_post, norm_ffn_pre, norm_ffn_post,
           hyb_w_in, hyb_w_out, dn_conv_w, dn_a_log, dn_dt_bias, dn_out_norm, att_q_norm, att_k_norm,
           sc_w_in, sc_conv_w, sc_w_out, ffn_w_up, ffn_conv_w, ffn_w_down):
    bsz, seq, d = x.shape
    depth = mod_w.shape[0]
    ctx_len = ctx.shape[1]
    fc = 256
    tm_glu, tm_proj, tm_out, tq = 512, 512, 512, 512

    rows = ((bsz + 1 + HALO - 1) // HALO) * HALO
    s_pad = jnp.zeros((rows, d), F32).at[:bsz].set(c).at[bsz].set(c_ctx)
    mod_all = _mod_call(s_pad, mod_w, mod_b).reshape(depth, rows, 6, d)

    cos_l, sin_l = _rope_tables(seq)
    cos_c, sin_c = jnp.ones((ctx_len, LANES), F32), jnp.zeros((ctx_len, LANES), F32)
    vec = lambda v: v.reshape(1, -1)

    for layer in range(depth):
        even = layer % 2 == 0
        ctx_live = any(j % 2 == 0 for j in range(layer + 1, depth))
        mod_l = mod_all[layer, :bsz]
        mod_c = mod_all[layer, bsz:bsz + 1]
        npre, npost = vec(norm_mix_pre[layer]), vec(norm_mix_post[layer])
        if even:
            e = layer // 2
            w, qn, kn, alog, dtb = _even_weights(hyb_w_in[e], att_q_norm[e], att_k_norm[e],
                                                 dn_a_log[e], dn_dt_bias[e])
            cw = dn_conv_w[e]
            w_out = hyb_w_out[e].astype(BF16)
            dnn = vec(dn_out_norm[e])
            proj = functools.partial(_proj_call, npre=npre, w=w, cw=cw, alog=alog, dtb=dtb, qn=qn, kn=kn,
                                     tm=tm_proj, chunk=DN_CHUNK)
            pl_ = proj(x, mod_l, cos_t=cos_l, sin_t=sin_l, q_side=True)
            pc_ = proj(ctx, mod_c, cos_t=cos_c, sin_t=sin_c, q_side=ctx_live)
            k_l, v_l, gc_l, gct_l, ak_l, avt_l, q_l, z_l, aq_l = pl_
            k_c, v_c, gc_c, gct_c, ak_c, avt_c = pc_[:6]
            lat_in = (k_l, v_l, gc_l, gct_l, q_l)
            ctx_in = (k_c, v_c, gc_c, gct_c) + ((pc_[6],) if ctx_live else ())
            o_dn = _dn_call(ctx_in, lat_in, chunk=DN_CHUNK, sup=DN_SUPER, hp=DN_HEADS_PER_STEP, ctx_q=ctx_live)
            b_lat = _attn_call(aq_l, [(ak_c, avt_c), (ak_l, avt_l)], tq=tq)
            x_new = _mixout_call(x, o_dn[0], z_l, b_lat, mod_l, dnn, npost, w_out, tm=tm_out)
            if ctx_live:
                b_ctx = _attn_call(pc_[8], [(ak_c, avt_c)], tq=tq)
                ctx = _mixout_call(ctx, o_dn[1], pc_[7], b_ctx, mod_c, dnn, npost, w_out, tm=tm_out)
            x = x_new
        else:
            o = layer // 2
            sc = functools.partial(_glu_call, npre=npre, npost=npost, win=sc_w_in[o].astype(BF16), cw=sc_conv_w[o],
                                   wout=sc_w_out[o].astype(BF16), mode="sc", rows=(0, 1, 2), tm=tm_glu, fc=fc)
            x = sc(x, mod_l)
            if ctx_live:
                ctx = sc(ctx, mod_c)
        ffn = functools.partial(_glu_call, npre=vec(norm_ffn_pre[layer]), npost=vec(norm_ffn_post[layer]),
                                win=ffn_w_up[layer].astype(BF16), cw=ffn_conv_w[layer],
                                wout=ffn_w_down[layer].astype(BF16), mode="ffn", rows=(3, 4, 5), tm=tm_glu, fc=fc)
        x = ffn(x, mod_l)
        if ctx_live:
            ctx = ffn(ctx, mod_c)
    return x
```

```python
import functools

import jax
import jax.numpy as jnp
from jax import lax
from jax.experimental import pallas as pl
from jax.experimental.pallas import tpu as pltpu

F32 = jnp.float32
BF16 = jnp.bfloat16
EPS = 1e-6
HALO = 8
LANES = 128
LOG2E = 1.4426950408889634

D_MODEL = 1024
GRID_W = 64
ROPE_THETA = 10000.0
DN_HEADS = 4
DN_DK = 128
DN_DIRS = 2
DN_CHUNK = 64
DN_SUPER = 256
DN_HEADS_PER_STEP = 4
ATT_HEADS = 8
ATT_KV_HEADS = 2
ATT_HD = 64
ATT_GROUPS = ATT_HEADS // ATT_KV_HEADS
D_FF = 2816
DN_W = DN_HEADS * DN_DK
ATT_Q_W = ATT_HEADS * ATT_HD
ATT_KV_W = ATT_KV_HEADS * ATT_HD

PC_K, PC_V, PC_Q, PC_Z, PC_AQ, PC_AK, PC_AV, PC_AB, PC_END = (
    0, 512, 1024, 1536, 2048, 2560, 2688, 2816, 2944)

VMEM_LIMIT = 56 * 1024 * 1024


def _resident(shape):
    nd = len(shape)
    return pl.BlockSpec(shape, lambda *_: (0,) * nd, pipeline_mode=pl.Buffered(1))


def _rms(x, w):
    return x * lax.rsqrt(jnp.mean(x * x, axis=-1, keepdims=True) + EPS) * w


def _silu(x):
    return x * (1.0 / (1.0 + jnp.exp(-x)))


def _dot(a, b):
    return jnp.dot(a, b, preferred_element_type=F32)


def _dot_nt(a, b):
    return lax.dot_general(a, b, (((1,), (1,)), ((), ())), preferred_element_type=F32)


def _dot_tn(a, b):
    return lax.dot_general(a, b, (((0,), (0,)), ((), ())), preferred_element_type=F32)


def _split3(x):
    hi = x.astype(BF16)
    r = x - hi.astype(F32)
    mid = r.astype(BF16)
    lo = (r - mid.astype(F32)).astype(BF16)
    return hi, mid, lo


def _valid_rows(tm):
    t = pl.program_id(1)
    nt = pl.num_programs(1)
    r = lax.broadcasted_iota(jnp.int32, (tm + 2 * HALO, 1), 0)
    lo_ok = jnp.logical_or(r >= HALO, t > 0)
    hi_ok = jnp.logical_or(r < tm + HALO, t < nt - 1)
    return jnp.logical_and(lo_ok, hi_ok)


def _conv3(cin, cw, tm):
    return (cw[0:1] * cin[HALO - 1:HALO - 1 + tm] + cw[1:2] * cin[HALO:HALO + tm]
            + cw[2:3] * cin[HALO + 1:HALO + 1 + tm])


def _halo_specs(tm, t_len, d):
    tb = tm // HALO
    nhb = t_len // HALO
    return [
        pl.BlockSpec((1, tm, d), lambda b, t: (b, t, 0)),
        pl.BlockSpec((1, HALO, d), lambda b, t: (b, jnp.maximum(t * tb - 1, 0), 0)),
        pl.BlockSpec((1, HALO, d), lambda b, t: (b, jnp.minimum((t + 1) * tb, nhb - 1), 0)),
    ]


def _mod_spec(mod):
    if mod.shape[0] > 1:
        return pl.BlockSpec((1, 6, mod.shape[2]), lambda b, t: (b, 0, 0))
    return pl.BlockSpec((1, 6, mod.shape[2]), lambda b, t: (0, 0, 0))


def _mod_kernel(s_ref, w_ref, b_ref, o_ref):
    s = _silu(s_ref[...])
    o_ref[0] = jnp.dot(s, w_ref[0], preferred_element_type=F32,
                       precision=lax.Precision.HIGHEST) + b_ref[0]


def _mod_call(s_pad, mod_w, mod_b):
    depth, d, n6 = mod_w.shape
    rows = s_pad.shape[0]
    nb = n6 // d
    return pl.pallas_call(
        _mod_kernel,
        out_shape=jax.ShapeDtypeStruct((depth, rows, n6), F32),
        grid=(depth, nb),
        in_specs=[
            pl.BlockSpec((rows, d), lambda l, j: (0, 0)),
            pl.BlockSpec((1, d, d), lambda l, j: (l, 0, j)),
            pl.BlockSpec((1, 1, d), lambda l, j: (l, 0, j)),
        ],
        out_specs=pl.BlockSpec((1, rows, d), lambda l, j: (l, 0, j)),
        compiler_params=pltpu.CompilerParams(
            dimension_semantics=("arbitrary", "arbitrary"), vmem_limit_bytes=VMEM_LIMIT),
        name="mod",
    )(s_pad, mod_w, mod_b.reshape(depth, 1, n6))


def _glu_kernel(xm_ref, xp_ref, xn_ref, mod_ref, npre_ref, npost_ref, win_ref, cw_ref, wout_ref,
                o_ref, g_ref, *, mode, tm, nf, fc, rows):
    sh_i, sc_i, gt_i = rows
    x_main = xm_ref[0]
    x_ext = jnp.concatenate([xp_ref[0], x_main, xn_ref[0]], axis=0)
    shift = mod_ref[0, sh_i:sh_i + 1, :]
    scale = mod_ref[0, sc_i:sc_i + 1, :]
    gate = mod_ref[0, gt_i:gt_i + 1, :]
    hb = (_rms(x_ext, npre_ref[...]) * (1.0 + scale) + shift).astype(BF16)
    valid = _valid_rows(tm)
    nbr = 2 if mode == "ffn" else 3
    f_all = nf * fc

    def up(j):
        return [_dot(hb, win_ref[:, b * f_all + j * fc:b * f_all + (j + 1) * fc]) for b in range(nbr)]

    def unit(j, p):
        cw = cw_ref[:, j * fc:(j + 1) * fc]
        if mode == "ffn":
            conv = _conv3(jnp.where(valid, p[0], 0.0), cw, tm)
            g = _silu(conv) * p[1][HALO:HALO + tm]
        else:
            conv = _conv3(jnp.where(valid, p[1] * p[2], 0.0), cw, tm)
            g = p[0][HALO:HALO + tm] * conv
        g_ref[:, j * fc:(j + 1) * fc] = g.astype(BF16)

    p = up(0)
    for j in range(nf):
        p_next = up(j + 1) if j + 1 < nf else None
        unit(j, p)
        p = p_next
    y = _dot(g_ref[...], wout_ref[...])
    o_ref[0] = x_main + gate * _rms(y, npost_ref[...])


def _glu_call(x, mod, npre, npost, win, cw, wout, *, mode, rows, tm, fc):
    bsz, t_len, d = x.shape
    f_all = wout.shape[0]
    assert f_all % fc == 0 and fc % LANES == 0
    nf = f_all // fc
    tm = min(tm, t_len)
    kern = functools.partial(_glu_kernel, mode=mode, tm=tm, nf=nf, fc=fc, rows=rows)
    return pl.pallas_call(
        kern,
        out_shape=jax.ShapeDtypeStruct(x.shape, F32),
        grid=(bsz, t_len // tm),
        in_specs=_halo_specs(tm, t_len, d) + [
            _mod_spec(mod),
            pl.BlockSpec((1, d), lambda b, t: (0, 0)),
            pl.BlockSpec((1, d), lambda b, t: (0, 0)),
            _resident(win.shape), _resident(cw.shape), _resident(wout.shape),
        ],
        out_specs=pl.BlockSpec((1, tm, d), lambda b, t: (b, t, 0)),
        scratch_shapes=[pltpu.VMEM((tm, f_all), BF16)],
        compiler_params=pltpu.CompilerParams(
            dimension_semantics=("parallel", "arbitrary"), vmem_limit_bytes=VMEM_LIMIT),
        name="glu_" + mode,
    )(x, x, x, mod, npre, npost, win, cw, wout)


def _proj_kernel(*refs, tm, chunk, q_side):
    (xm_ref, xp_ref, xn_ref, mod_ref, npre_ref, w_ref, cw_ref, alog_ref, dtb_ref,
     qn_ref, kn_ref, cos_ref, sin_ref) = refs[:13]
    outs = refs[13:]
    k_o, v_o, gc_o, gcT_o, ak_o, avT_o = outs[:6]
    if q_side:
        q_o, z_o, aq_o = outs[6:9]

    x_main = xm_ref[0]
    x_ext = jnp.concatenate([xp_ref[0], x_main, xn_ref[0]], axis=0)
    shift = mod_ref[0, 0:1, :]
    scale = mod_ref[0, 1:2, :]
    npre = npre_ref[...]
    hb_ext = (_rms(x_ext, npre) * (1.0 + scale) + shift).astype(BF16)
    hb = (_rms(x_main, npre) * (1.0 + scale) + shift).astype(BF16)
    valid = _valid_rows(tm)

    groups = [(PC_K, k_o, True, 1.0), (PC_V, v_o, False, 1.0)]
    if q_side:
        groups.append((PC_Q, q_o, True, DN_DK ** -0.5))
    for c0, o_ref, unit, mult in groups:
        p = jnp.where(valid, _dot(hb_ext, w_ref[:, c0:c0 + DN_W]), 0.0)
        a = _silu(_conv3(p, cw_ref[:, c0:c0 + DN_W], tm))
        if unit:
            for hd in range(DN_HEADS):
                seg = a[:, hd * DN_DK:(hd + 1) * DN_DK]
                inv = lax.rsqrt(jnp.sum(seg * seg, axis=-1, keepdims=True) + EPS)
                o_ref[0, :, hd * DN_DK:(hd + 1) * DN_DK] = (seg * (inv * mult)).astype(BF16)
        else:
            o_ref[0] = a.astype(BF16)

    lane = lax.broadcasted_iota(jnp.int32, (1, LANES), 1)
    half = (lane % ATT_HD) < (ATT_HD // 2)
    cos_t = cos_ref[...]
    sin_t = sin_ref[...]
    r_i = lax.broadcasted_iota(jnp.int32, (LANES, LANES), 0)
    c_i = lax.broadcasted_iota(jnp.int32, (LANES, LANES), 1)
    head_mean = jnp.where((r_i // ATT_HD) == (c_i // ATT_HD), 1.0 / ATT_HD, 0.0).astype(BF16)

    def head_norm_rope(xa, wn):
        ms = _dot((xa * xa).astype(BF16), head_mean)
        xa = xa * lax.rsqrt(ms + EPS) * wn
        swapped = jnp.where(half, pltpu.roll(xa, LANES - ATT_HD // 2, 1), pltpu.roll(xa, ATT_HD // 2, 1))
        return xa * cos_t + swapped * sin_t

    if q_side:
        z_o[0] = _dot(hb, w_ref[:, PC_Z:PC_AQ])
        pq = _dot(hb, w_ref[:, PC_AQ:PC_AK])
        for gq in range(ATT_Q_W // LANES):
            xa = head_norm_rope(pq[:, gq * LANES:(gq + 1) * LANES], qn_ref[...])
            aq_o[0, gq * LANES:(gq + 1) * LANES, :] = (xa * (ATT_HD ** -0.5 * LOG2E)).T.astype(BF16)

    pk = _dot(hb, w_ref[:, PC_AK:PC_END])
    ak = head_norm_rope(pk[:, :LANES], kn_ref[...])
    ak_o[0] = ak.astype(BF16)
    avT_o[0] = pk[:, LANES:2 * LANES].T.astype(BF16)

    pab = pk[:, 2 * LANES:]
    n_dh = DN_DIRS * DN_HEADS
    zab = pab + dtb_ref[...]
    softplus = jnp.maximum(zab, 0.0) + jnp.log(1.0 + jnp.exp(-jnp.abs(zab)))
    g = -jnp.exp(alog_ref[...]) * softplus
    beta = 1.0 / (1.0 + jnp.exp(-pab))
    g = jnp.where(lane < n_dh, g, 0.0)
    half_rows = min(tm, 2 * LANES)
    rr = lax.broadcasted_iota(jnp.int32, (half_rows, half_rows), 0)
    cc = lax.broadcasted_iota(jnp.int32, (half_rows, half_rows), 1)
    low = jnp.where(jnp.logical_and(rr // chunk == cc // chunk, cc <= rr), 1.0, 0.0).astype(BF16)
    pres = []
    for r0 in range(0, tm, half_rows):
        parts = jnp.concatenate(_split3(g[r0:r0 + half_rows]), axis=1)
        acc = _dot(low, parts)
        pres.append(acc[:, :LANES] + acc[:, LANES:2 * LANES] + acc[:, 2 * LANES:])
    pre = jnp.concatenate(pres, axis=0)
    total = jnp.concatenate([jnp.broadcast_to(pre[r + chunk - 1:r + chunk], (chunk, LANES))
                             for r in range(0, tm, chunk)], axis=0)
    suf = total - pre + g
    gc = jnp.where(lane < DN_HEADS, pre, jnp.where(lane < n_dh, suf, jnp.where(lane < 2 * n_dh, beta, 0.0)))
    gc_o[0] = gc
    gcT_o[0] = gc.T[0:2 * n_dh, :]


def _proj_call(x, mod, npre, w, cw, alog, dtb, qn, kn, cos_t, sin_t, *, q_side, tm, chunk):
    bsz, t_len, d = x.shape
    tm = min(tm, t_len)
    nt = t_len // tm
    f32_out = lambda n: jax.ShapeDtypeStruct((bsz, t_len, n), F32)
    bf16_out = lambda n: jax.ShapeDtypeStruct((bsz, t_len, n), BF16)
    row_spec = lambda n: pl.BlockSpec((1, tm, n), lambda b, t: (b, t, 0))
    out_shape = [bf16_out(DN_W), bf16_out(DN_W), f32_out(LANES),
                 jax.ShapeDtypeStruct((bsz, 2 * DN_DIRS * DN_HEADS, t_len), F32),
                 bf16_out(ATT_KV_W),
                 jax.ShapeDtypeStruct((bsz, ATT_KV_W, t_len), BF16)]
    out_specs = [row_spec(DN_W), row_spec(DN_W), row_spec(LANES),
                 pl.BlockSpec((1, 2 * DN_DIRS * DN_HEADS, tm), lambda b, t: (b, 0, t)),
                 row_spec(ATT_KV_W),
                 pl.BlockSpec((1, ATT_KV_W, tm), lambda b, t: (b, 0, t))]
    if q_side:
        out_shape += [bf16_out(DN_W), f32_out(DN_W), jax.ShapeDtypeStruct((bsz, ATT_Q_W, t_len), BF16)]
        out_specs += [row_spec(DN_W), row_spec(DN_W), pl.BlockSpec((1, ATT_Q_W, tm), lambda b, t: (b, 0, t))]
    vec = lambda n: pl.BlockSpec((1, n), lambda b, t: (0, 0))
    kern = functools.partial(_proj_kernel, tm=tm, chunk=chunk, q_side=q_side)
    return pl.pallas_call(
        kern,
        out_shape=out_shape,
        grid=(bsz, nt),
        in_specs=_halo_specs(tm, t_len, d) + [
            _mod_spec(mod), vec(d), _resident(w.shape), _resident(cw.shape),
            vec(LANES), vec(LANES), vec(LANES), vec(LANES),
            pl.BlockSpec((tm, LANES), lambda b, t: (t, 0)),
            pl.BlockSpec((tm, LANES), lambda b, t: (t, 0)),
        ],
        out_specs=out_specs,
        compiler_params=pltpu.CompilerParams(
            dimension_semantics=("parallel", "arbitrary"), vmem_limit_bytes=VMEM_LIMIT),
        name="proj_q" if q_side else "proj_kv",
    )(x, x, x, mod, npre, w, cw, alog, dtb, qn, kn, cos_t, sin_t)


def _dn_kernel(*refs, chunk, sup, n_ctx, n_lat, ctx_q, hp):
    c = chunk
    nb = sup // c
    n_in_ctx = 5 if ctx_q else 4
    ctx_in = refs[:n_in_ctx]
    lat_in = refs[n_in_ctx:n_in_ctx + 5]
    n_out = 2 if ctx_q else 1
    outs = refs[n_in_ctx + 5:n_in_ctx + 5 + n_out]
    u_s, w_s, kd_s, qg_s, in_s, eg_s = refs[n_in_ctx + 5 + n_out:]
    head0 = pl.program_id(1) * hp
    chains = [(hh, d) for hh in range(hp) for d in range(DN_DIRS)]
    n_dh = DN_DIRS * DN_HEADS

    ri = lax.broadcasted_iota(jnp.int32, (sup, sup), 0)
    ci = lax.broadcasted_iota(jnp.int32, (sup, sup), 1)
    same = (ri // c) == (ci // c)
    incl = [jnp.logical_and(same, ci <= ri), jnp.logical_and(same, ci >= ri)]
    strict = [jnp.logical_and(same, ci < ri), jnp.logical_and(same, ci > ri)]
    rw = lax.broadcasted_iota(jnp.int32, (c, sup), 0)
    cw = lax.broadcasted_iota(jnp.int32, (c, sup), 1)
    cl = cw % c
    lane_blk = cw // c
    eye_w = jnp.where(rw == cl, 1.0, 0.0)
    level_mask = []
    n = 1
    while n < c:
        level_mask.append(jnp.logical_and(rw // (2 * n) == cl // (2 * n), rw // n != cl // n))
        n *= 2
    lane = lax.broadcasted_iota(jnp.int32, (1, LANES), 1)
    sub = lax.broadcasted_iota(jnp.int32, (2 * n_dh, 1), 0)

    def pick_col(x, idx):
        return jnp.sum(jnp.where(lane == idx, x, 0.0), axis=-1, keepdims=True)

    def block_diag(xw):
        return jnp.where(same, jnp.concatenate([xw] * nb, axis=0), 0.0).astype(BF16)

    def prep(slot, src, pos, has_q):
        if has_q:
            k_ref, v_ref, gc_ref, gct_ref, q_ref = src
        else:
            k_ref, v_ref, gc_ref, gct_ref = src
        r0 = [pl.multiple_of(pos[d] * sup, sup) for d in range(DN_DIRS)]
        gcb = [gc_ref[0, pl.ds(r0[d], sup), :] for d in range(DN_DIRS)]
        gct = [gct_ref[0, :, pl.ds(r0[d], sup)] for d in range(DN_DIRS)]
        st = []
        for n, (hh, d) in enumerate(chains):
            hs = slice(hh * DN_DK, (hh + 1) * DN_DK)
            k16 = k_ref[0, pl.ds(r0[d], sup), hs]
            k = k16.astype(F32)
            v = v_ref[0, pl.ds(r0[d], sup), hs].astype(F32)
            q = q_ref[0, pl.ds(r0[d], sup), hs].astype(F32) if has_q else None
            col = d * DN_HEADS + head0 + hh
            gc_col = pick_col(gcb[d], col)
            beta = pick_col(gcb[d], n_dh + col)
            gc_row = jnp.sum(jnp.where(sub == col, gct[d], 0.0), axis=0, keepdims=True)
            decay = jnp.where(incl[d], jnp.exp(jnp.where(incl[d], gc_col - gc_row, 0.0)), 0.0)
            kbeta = k * beta
            if has_q:
                kk = _dot_nt(jnp.concatenate([kbeta, q], axis=0).astype(BF16), k16)
                low = jnp.where(strict[d], kk[:sup] * decay, 0.0)
                intra = kk[sup:] * decay
                for b in range(nb):
                    in_s[slot, n, b * c:(b + 1) * c, :] = intra[b * c:(b + 1) * c, b * c:(b + 1) * c].astype(BF16)
            else:
                low = jnp.where(strict[d], _dot_nt(kbeta.astype(BF16), k16) * decay, 0.0)
            low_w = jnp.where(lane_blk == 0, low[0:c], 0.0)
            for b in range(1, nb):
                low_w = low_w + jnp.where(lane_blk == b, low[b * c:(b + 1) * c], 0.0)
            egc = jnp.exp(gc_col)
            rhs = jnp.concatenate([v * beta, kbeta * egc], axis=1).astype(BF16)
            last = [b * c + (c - 1 if d == 0 else 0) for b in range(nb)]
            g_last = jnp.concatenate([jnp.broadcast_to(gc_col[i:i + 1], (c, 1)) for i in last], axis=0)
            kd_s[slot, n] = (k * jnp.exp(g_last - gc_col)).astype(BF16)
            for b, i in enumerate(last):
                eg_s[slot, n, b] = jnp.broadcast_to(jnp.exp(gc_col[i:i + 1]), (HALO, LANES))
            if has_q:
                qg_s[slot, n] = (q * egc).astype(BF16)
            st.append((low_w, rhs))
            yield

        tw = [eye_w - jnp.where(level_mask[0], s[0], 0.0) for s in st]
        for lm in level_mask[1:]:
            et = [_dot(jnp.where(lm, s[0], 0.0).astype(BF16), block_diag(t)) for s, t in zip(st, tw)]
            yield
            tw = [t - _dot(t.astype(BF16), block_diag(e)) for t, e in zip(tw, et)]
            yield

        for n, (s, t) in enumerate(zip(st, tw)):
            uw = _dot(block_diag(t), s[1])
            u_s[slot, n] = uw[:, :DN_DK]
            w_s[slot, n] = uw[:, DN_DK:].astype(BF16)
            yield

    def scan_step(slot, t, states, rows_out, o_ref, has_q):
        new = []
        rows = [slice(t * c, (t + 1) * c), slice((nb - 1 - t) * c, (nb - t) * c)]
        blk = [t, nb - 1 - t]
        s16 = [s.astype(BF16) for s in states]
        if has_q:
            ws = [_dot(jnp.concatenate([w_s[slot, n, rows[d], :], qg_s[slot, n, rows[d], :]], axis=0), s16[n])
                  for n, (hh, d) in enumerate(chains)]
            v16 = [(u_s[slot, n, rows[d], :] - ws[n][:c]).astype(BF16) for n, (hh, d) in enumerate(chains)]
            for n, (hh, d) in enumerate(chains):
                o = ws[n][c:] + _dot(in_s[slot, n, rows[d], :], v16[n])
                dst = pl.ds(pl.multiple_of(rows_out[d] + blk[d] * c, c), c)
                o_ref[0, dst, hh * DN_DK:(hh + 1) * DN_DK] += o
        else:
            v16 = [(u_s[slot, n, rows[d], :] - _dot(w_s[slot, n, rows[d], :], s16[n])).astype(BF16)
                   for n, (hh, d) in enumerate(chains)]
        for n, (hh, d) in enumerate(chains):
            new.append(states[n] * eg_s[slot, n, blk[d], 0:1, :] + _dot_tn(kd_s[slot, n, rows[d], :], v16[n]))
        return tuple(new)

    def run(prep_gen, slot, states, rows_out, o_ref, has_q):
        holder = [states]
        todo = list(range(nb)) if slot is not None else []
        if prep_gen is not None:
            n_stage = 2 * len(chains) + 2 * (len(level_mask) - 1)
            every = max(n_stage // (nb + 1), 1)
            for i, _ in enumerate(prep_gen):
                if todo and (i + 1) % every == 0:
                    holder[0] = scan_step(slot, todo.pop(0), holder[0], rows_out, o_ref, has_q)
        for t in todo:
            holder[0] = scan_step(slot, t, holder[0], rows_out, o_ref, has_q)
        return holder[0]

    for o_ref in outs:
        o_ref[...] = jnp.zeros_like(o_ref)

    zero = jnp.zeros((DN_DK, DN_DK), F32)
    states = tuple(zero for _ in chains)

    def ctx_pos(p):
        return (p, n_ctx - 1 - p)

    def lat_pos(p):
        return (p, n_lat - 1 - p)

    def rows_of(pos):
        return [pos[0] * sup, pos[1] * sup]

    o_ctx = outs[1] if ctx_q else None
    run(prep(0, ctx_in, ctx_pos(0), ctx_q), None, states, None, None, False)
    for p in range(n_ctx):
        if p + 1 < n_ctx:
            nxt = prep((p + 1) % 2, ctx_in, ctx_pos(p + 1), ctx_q)
        else:
            nxt = prep((p + 1) % 2, lat_in, lat_pos(0), True)
        states = run(nxt, p % 2, states, rows_of(ctx_pos(p)), o_ctx, ctx_q)

    def lat_step(p, slot, states, last=False):
        nxt = None if last else prep(1 - slot, lat_in, lat_pos(p + 1), True)
        return run(nxt, slot, states, rows_of(lat_pos(p)), outs[0], True)

    s0 = n_ctx % 2
    n_loop = (n_lat - 1) // 2

    def lat_body(k, states):
        states = lat_step(2 * k, s0, states)
        return lat_step(2 * k + 1, 1 - s0, states)

    states = lax.fori_loop(0, n_loop, lat_body, states)
    for p in range(2 * n_loop, n_lat):
        states = lat_step(p, (s0 + p) % 2, states, last=p == n_lat - 1)


def _dn_call(ctx_in, lat_in, *, chunk, sup, hp, ctx_q):
    bsz, t_lat, _ = lat_in[0].shape
    t_ctx = ctx_in[0].shape[1]
    assert t_ctx % sup == 0 and t_lat % sup == 0 and sup % chunk == 0 and DN_HEADS % hp == 0
    n_ctx, n_lat = t_ctx // sup, t_lat // sup
    n_rows = 2 * DN_DIRS * DN_HEADS
    nb = sup // chunk

    def specs(t_len, has_q):
        head_blk = pl.BlockSpec((1, t_len, hp * DN_DK), lambda b, h: (b, 0, h))
        sp = [head_blk, head_blk,
              pl.BlockSpec((1, t_len, LANES), lambda b, h: (b, 0, 0)),
              pl.BlockSpec((1, n_rows, t_len), lambda b, h: (b, 0, 0))]
        if has_q:
            sp.append(head_blk)
        return sp

    out_shape = [jax.ShapeDtypeStruct((bsz, t_lat, DN_W), F32)]
    out_specs = [pl.BlockSpec((1, t_lat, hp * DN_DK), lambda b, h: (b, 0, h))]
    if ctx_q:
        out_shape.append(jax.ShapeDtypeStruct((bsz, t_ctx, DN_W), F32))
        out_specs.append(pl.BlockSpec((1, t_ctx, hp * DN_DK), lambda b, h: (b, 0, h)))
    nch = hp * DN_DIRS
    scratch = [
        pltpu.VMEM((2, nch, sup, DN_DK), F32),
        pltpu.VMEM((2, nch, sup, DN_DK), BF16),
        pltpu.VMEM((2, nch, sup, DN_DK), BF16),
        pltpu.VMEM((2, nch, sup, DN_DK), BF16),
        pltpu.VMEM((2, nch, sup, chunk), BF16),
        pltpu.VMEM((2, nch, nb, HALO, LANES), F32),
    ]
    kern = functools.partial(_dn_kernel, chunk=chunk, sup=sup, n_ctx=n_ctx, n_lat=n_lat, ctx_q=ctx_q, hp=hp)
    return pl.pallas_call(
        kern,
        out_shape=out_shape,
        grid=(bsz, DN_HEADS // hp),
        in_specs=specs(t_ctx, ctx_q) + specs(t_lat, True),
        out_specs=out_specs,
        scratch_shapes=scratch,
        compiler_params=pltpu.CompilerParams(
            dimension_semantics=("parallel", "arbitrary"), vmem_limit_bytes=VMEM_LIMIT),
        name="dn_q" if ctx_q else "dn_kv",
    )(*ctx_in, *lat_in)


def _attn_kernel(*refs, n_sets):
    q_ref = refs[0]
    kv = refs[1:1 + 2 * n_sets]
    o_ref = refs[1 + 2 * n_sets]
    slab = 64

    def fold(x, op):
        acc = x[0:slab]
        for r in range(slab, x.shape[0], slab):
            acc = op(acc, x[r:r + slab])
        return acc

    def head_scores(h):
        g = h // ATT_GROUPS
        q_t = q_ref[0, h * ATT_HD:(h + 1) * ATT_HD, :]
        zero = jnp.zeros_like(q_t)
        q_pad = jnp.concatenate([q_t, zero] if g == 0 else [zero, q_t], axis=0)
        return [_dot(kv[2 * i][0], q_pad) for i in range(n_sets)]

    def head_out(h, ps, den):
        g = h // ATT_GROUPS
        acc = _dot(kv[1][0, g * ATT_HD:(g + 1) * ATT_HD, :], ps[0])
        for i in range(1, n_sets):
            acc = acc + _dot(kv[2 * i + 1][0, g * ATT_HD:(g + 1) * ATT_HD, :], ps[i])
        return acc / den

    outs = []
    scores = head_scores(0)
    prev = None
    for h in range(ATT_HEADS):
        nxt = head_scores(h + 1) if h + 1 < ATT_HEADS else None
        if prev is not None:
            outs.append(head_out(h - 1, *prev))
        m = fold(scores[0], jnp.maximum)
        for s in scores[1:]:
            m = jnp.maximum(m, fold(s, jnp.maximum))
        m = m.max(axis=0, keepdims=True)
        ps = [jnp.exp2(s - m) for s in scores]
        den = fold(ps[0], jnp.add)
        for p in ps[1:]:
            den = den + fold(p, jnp.add)
        den = den.sum(axis=0, keepdims=True)
        prev = ([p.astype(BF16) for p in ps], den)
        scores = nxt
    outs.append(head_out(ATT_HEADS - 1, *prev))
    o_ref[0] = jnp.concatenate(outs, axis=0).T.astype(BF16)


def _attn_call(q_t, kv_sets, *, tq):
    bsz, _, t_len = q_t.shape
    tq = min(tq, t_len)
    in_specs = [pl.BlockSpec((1, ATT_Q_W, tq), lambda b, t: (b, 0, t))]
    args = [q_t]
    for k, v_t in kv_sets:
        s_len = k.shape[1]
        in_specs += [pl.BlockSpec((1, s_len, ATT_KV_W), lambda b, t: (b, 0, 0)),
                     pl.BlockSpec((1, ATT_KV_W, s_len), lambda b, t: (b, 0, 0))]
        args += [k, v_t]
    return pl.pallas_call(
        functools.partial(_attn_kernel, n_sets=len(kv_sets)),
        out_shape=jax.ShapeDtypeStruct((bsz, t_len, ATT_Q_W), BF16),
        grid=(bsz, t_len // tq),
        in_specs=in_specs,
        out_specs=pl.BlockSpec((1, tq, ATT_Q_W), lambda b, t: (b, t, 0)),
        compiler_params=pltpu.CompilerParams(
            dimension_semantics=("parallel", "arbitrary"), vmem_limit_bytes=VMEM_LIMIT),
        name="attn%d" % len(kv_sets),
    )(*args)


def _mixout_kernel(x_ref, o_ref, z_ref, b_ref, mod_ref, dnn_ref, npost_ref, w_ref, out_ref):
    gate = mod_ref[0, 2:3, :]
    o = o_ref[0]
    z = z_ref[0]
    parts = []
    for hd in range(DN_HEADS):
        seg = o[:, hd * DN_DK:(hd + 1) * DN_DK]
        parts.append(_rms(seg, dnn_ref[...]) * _silu(z[:, hd * DN_DK:(hd + 1) * DN_DK]))
    a = jnp.concatenate(parts, axis=1).astype(BF16)
    y = _dot(a, w_ref[0:DN_W, :]) + _dot(b_ref[0], w_ref[DN_W:, :])
    out_ref[0] = x_ref[0] + gate * _rms(y, npost_ref[...])


def _mixout_call(x, o_dn, z, b_att, mod, dnn, npost, w_out, *, tm):
    bsz, t_len, d = x.shape
    tm = min(tm, t_len)
    row = lambda n: pl.BlockSpec((1, tm, n), lambda b, t: (b, t, 0))
    return pl.pallas_call(
        _mixout_kernel,
        out_shape=jax.ShapeDtypeStruct(x.shape, F32),
        grid=(bsz, t_len // tm),
        in_specs=[row(d), row(DN_W), row(DN_W), row(ATT_Q_W), _mod_spec(mod),
                  pl.BlockSpec((1, DN_DK), lambda b, t: (0, 0)),
                  pl.BlockSpec((1, d), lambda b, t: (0, 0)),
                  _resident(w_out.shape)],
        out_specs=row(d),
        compiler_params=pltpu.CompilerParams(
            dimension_semantics=("parallel", "arbitrary"), vmem_limit_bytes=VMEM_LIMIT),
        name="mixout",
    )(x, o_dn, z, b_att, mod, dnn, npost, w_out)


def _rope_tables(n):
    half = ATT_HD // 2
    r = (jnp.arange(n) // GRID_W).astype(F32)
    col = (jnp.arange(n) % GRID_W).astype(F32)
    inv = ROPE_THETA ** (-jnp.arange(0, half, 2, dtype=F32) / half)
    ang = jnp.concatenate([r[:, None] * inv, col[:, None] * inv], axis=-1)
    cos, sin = jnp.cos(ang), jnp.sin(ang)
    cos_t = jnp.tile(jnp.concatenate([cos, cos], axis=-1), (1, LANES // ATT_HD))
    sin_t = jnp.tile(jnp.concatenate([-sin, sin], axis=-1), (1, LANES // ATT_HD))
    return cos_t, sin_t


def _even_weights(w_in, q_norm, k_norm, a_log, dt_bias):
    d = w_in.shape[0]
    n_ab = 2 * DN_DIRS * DN_HEADS
    off_ab = 2 * DN_W
    off_ak = off_ab + n_ab
    off_av = off_ak + ATT_KV_W
    off_q = off_av + ATT_KV_W
    off_z = off_q + DN_W
    off_aq = off_z + DN_W

    def deinterleave(a, heads):
        lead = a.shape[:-1]
        a = a.reshape(lead + (heads, ATT_HD // 2, 2))
        return jnp.swapaxes(a, -1, -2).reshape(lead + (heads * ATT_HD,))

    w = jnp.concatenate([
        w_in[:, :off_ab], w_in[:, off_q:off_aq],
        deinterleave(w_in[:, off_aq:off_aq + ATT_Q_W], ATT_HEADS),
        deinterleave(w_in[:, off_ak:off_av], ATT_KV_HEADS),
        w_in[:, off_av:off_q], w_in[:, off_ab:off_ak],
        jnp.zeros((d, PC_END - PC_AB - n_ab), w_in.dtype)], axis=1).astype(BF16)
    tile2 = lambda v: jnp.tile(deinterleave(v, 1), LANES // ATT_HD).reshape(1, LANES)
    pad = lambda v: jnp.pad(v.reshape(-1), (0, LANES - v.size)).reshape(1, LANES)
    return w, tile2(q_norm), tile2(k_norm), pad(a_log), pad(dt_bias)


def kernel(x, c, ctx, c_ctx, mod_w, mod_b, norm_mix_pre, norm_mix_post, norm_ffn_pre, norm_ffn_post,
           hyb_w_in, hyb_w_out, dn_conv_w, dn_a_log, dn_dt_bias, dn_out_norm, att_q_norm, att_k_norm,
           sc_w_in, sc_conv_w, sc_w_out, ffn_w_up, ffn_conv_w, ffn_w_down):
    bsz, seq, d = x.shape
    depth = mod_w.shape[0]
    ctx_len = ctx.shape[1]
    fc = 256
    tm_glu, tm_proj, tm_out, tq = 512, 512, 512, 512

    rows = ((bsz + 1 + HALO - 1) // HALO) * HALO
    s_pad = jnp.zeros((rows, d), F32).at[:bsz].set(c).at[bsz].set(c_ctx)
    mod_all = _mod_call(s_pad, mod_w, mod_b).reshape(depth, rows, 6, d)

    cos_l, sin_l = _rope_tables(seq)
    cos_c, sin_c = jnp.ones((ctx_len, LANES), F32), jnp.zeros((ctx_len, LANES), F32)
    vec = lambda v: v.reshape(1, -1)

    for layer in range(depth):
        even = layer % 2 == 0
        ctx_live = any(j % 2 == 0 for j in range(layer + 1, depth))
        mod_l = mod_all[layer, :bsz]
        mod_c = mod_all[layer, bsz:bsz + 1]
        npre, npost = vec(norm_mix_pre[layer]), vec(norm_mix_post[layer])
        if even:
            e = layer // 2
            w, qn, kn, alog, dtb = _even_weights(hyb_w_in[e], att_q_norm[e], att_k_norm[e],
                                                 dn_a_log[e], dn_dt_bias[e])
            cw = dn_conv_w[e]
            w_out = hyb_w_out[e].astype(BF16)
            dnn = vec(dn_out_norm[e])
            proj = functools.partial(_proj_call, npre=npre, w=w, cw=cw, alog=alog, dtb=dtb, qn=qn, kn=kn,
                                     tm=tm_proj, chunk=DN_CHUNK)
            pl_ = proj(x, mod_l, cos_t=cos_l, sin_t=sin_l, q_side=True)
            pc_ = proj(ctx, mod_c, cos_t=cos_c, sin_t=sin_c, q_side=ctx_live)
            k_l, v_l, gc_l, gct_l, ak_l, avt_l, q_l, z_l, aq_l = pl_
            k_c, v_c, gc_c, gct_c, ak_c, avt_c = pc_[:6]
            lat_in = (k_l, v_l, gc_l, gct_l, q_l)
            ctx_in = (k_c, v_c, gc_c, gct_c) + ((pc_[6],) if ctx_live else ())
            o_dn = _dn_call(ctx_in, lat_in, chunk=DN_CHUNK, sup=DN_SUPER, hp=DN_HEADS_PER_STEP, ctx_q=ctx_live)
            b_lat = _attn_call(aq_l, [(ak_c, avt_c), (ak_l, avt_l)], tq=tq)
            x_new = _mixout_call(x, o_dn[0], z_l, b_lat, mod_l, dnn, npost, w_out, tm=tm_out)
            if ctx_live:
                b_ctx = _attn_call(pc_[8], [(ak_c, avt_c)], tq=tq)
                ctx = _mixout_call(ctx, o_dn[1], pc_[7], b_ctx, mod_c, dnn, npost, w_out, tm=tm_out)
            x = x_new
        else:
            o = layer // 2
            sc = functools.partial(_glu_call, npre=npre, npost=npost, win=sc_w_in[o].astype(BF16), cw=sc_conv_w[o],
                                   wout=sc_w_out[o].astype(BF16), mode="sc", rows=(0, 1, 2), tm=tm_glu, fc=fc)
            x = sc(x, mod_l)
            if ctx_live:
                ctx = sc(ctx, mod_c)
        ffn = functools.partial(_glu_call, npre=vec(norm_ffn_pre[layer]), npost=vec(norm_ffn_post[layer]),
                                win=ffn_w_up[layer].astype(BF16), cw=ffn_conv_w[layer],
                                wout=ffn_w_down[layer].astype(BF16), mode="ffn", rows=(3, 4, 5), tm=tm_glu, fc=fc)
        x = ffn(x, mod_l)
        if ctx_live:
            ctx = ffn(ctx, mod_c)
    return x
```

```python
import functools

import jax
import jax.numpy as jnp
from jax import lax
from jax.experimental import pallas as pl
from jax.experimental.pallas import tpu as pltpu

F32 = jnp.float32
BF16 = jnp.bfloat16
EPS = 1e-6
HALO = 8
LANES = 128
LOG2E = 1.4426950408889634

GRID_W = 64
ROPE_THETA = 10000.0
DN_HEADS = 4
DN_DK = 128
DN_DIRS = 2
DN_CHUNK = 64
DN_SUPER = 256
DN_HEADS_PER_STEP = 4
ATT_HEADS = 8
ATT_KV_HEADS = 2
ATT_HD = 64
ATT_GROUPS = ATT_HEADS // ATT_KV_HEADS
DN_W = DN_HEADS * DN_DK
ATT_Q_W = ATT_HEADS * ATT_HD
ATT_KV_W = ATT_KV_HEADS * ATT_HD

PC_K, PC_V, PC_Q, PC_Z, PC_AQ, PC_AK, PC_AV, PC_AB, PC_END = (
    0, 512, 1024, 1536, 2048, 2560, 2688, 2816, 2944)

VMEM_LIMIT = 56 * 1024 * 1024

TM_GLU, TM_SC, TM_PROJ, TM_OUT, TQ_ATTN, FC_GLU = 512, 1024, 512, 512, 512, 256


def _resident(shape):
    nd = len(shape)
    return pl.BlockSpec(shape, lambda *_: (0,) * nd, pipeline_mode=pl.Buffered(1))


def _rms(x, w):
    return x * lax.rsqrt(jnp.mean(x * x, axis=-1, keepdims=True) + EPS) * w


def _silu(x):
    return x * (1.0 / (1.0 + jnp.exp(-x)))


def _dot(a, b):
    return jnp.dot(a, b, preferred_element_type=F32)


def _dot_nt(a, b):
    return lax.dot_general(a, b, (((1,), (1,)), ((), ())), preferred_element_type=F32)


def _dot_tn(a, b):
    return lax.dot_general(a, b, (((0,), (0,)), ((), ())), preferred_element_type=F32)


def _split3(x):
    hi = x.astype(BF16)
    r = x - hi.astype(F32)
    mid = r.astype(BF16)
    lo = (r - mid.astype(F32)).astype(BF16)
    return hi, mid, lo


def _valid_rows(tm):
    t = pl.program_id(1)
    nt = pl.num_programs(1)
    r = lax.broadcasted_iota(jnp.int32, (tm + 2 * HALO, 1), 0)
    lo_ok = jnp.logical_or(r >= HALO, t > 0)
    hi_ok = jnp.logical_or(r < tm + HALO, t < nt - 1)
    return jnp.logical_and(lo_ok, hi_ok)


def _conv3(cin, cw, tm):
    return (cw[0:1] * cin[HALO - 1:HALO - 1 + tm] + cw[1:2] * cin[HALO:HALO + tm]
            + cw[2:3] * cin[HALO + 1:HALO + 1 + tm])


def _halo_specs(tm, t_len, d):
    tb = tm // HALO
    nhb = t_len // HALO
    return [
        pl.BlockSpec((1, tm, d), lambda b, t: (b, t, 0)),
        pl.BlockSpec((1, HALO, d), lambda b, t: (b, jnp.maximum(t * tb - 1, 0), 0)),
        pl.BlockSpec((1, HALO, d), lambda b, t: (b, jnp.minimum((t + 1) * tb, nhb - 1), 0)),
    ]


def _mod_spec(mod):
    if mod.shape[0] > 1:
        return pl.BlockSpec((1, 6, mod.shape[2]), lambda b, t: (b, 0, 0))
    return pl.BlockSpec((1, 6, mod.shape[2]), lambda b, t: (0, 0, 0))


def _mod_kernel(s_ref, w_ref, b_ref, o_ref):
    s = _silu(s_ref[...])
    o_ref[0] = jnp.dot(s, w_ref[0], preferred_element_type=F32,
                       precision=lax.Precision.HIGHEST) + b_ref[0]


def _mod_call(s_pad, mod_w, mod_b):
    depth, d, n6 = mod_w.shape
    rows = s_pad.shape[0]
    nb = n6 // d
    return pl.pallas_call(
        _mod_kernel,
        out_shape=jax.ShapeDtypeStruct((depth, rows, n6), F32),
        grid=(depth, nb),
        in_specs=[
            pl.BlockSpec((rows, d), lambda l, j: (0, 0)),
            pl.BlockSpec((1, d, d), lambda l, j: (l, 0, j)),
            pl.BlockSpec((1, 1, d), lambda l, j: (l, 0, j)),
        ],
        out_specs=pl.BlockSpec((1, rows, d), lambda l, j: (l, 0, j)),
        compiler_params=pltpu.CompilerParams(
            dimension_semantics=("arbitrary", "arbitrary"), vmem_limit_bytes=VMEM_LIMIT),
        name="mod",
    )(s_pad, mod_w, mod_b.reshape(depth, 1, n6))


def _glu_kernel(xm_ref, xp_ref, xn_ref, mod_ref, npre_ref, npost_ref, win_ref, cw_ref, wout_ref,
                o_ref, g_ref, *, mode, tm, nf, fc, rows):
    sh_i, sc_i, gt_i = rows
    x_main = xm_ref[0]
    x_ext = jnp.concatenate([xp_ref[0], x_main, xn_ref[0]], axis=0)
    shift = mod_ref[0, sh_i:sh_i + 1, :]
    scale = mod_ref[0, sc_i:sc_i + 1, :]
    gate = mod_ref[0, gt_i:gt_i + 1, :]
    hb = (_rms(x_ext, npre_ref[...]) * (1.0 + scale) + shift).astype(BF16)
    valid = _valid_rows(tm)
    nbr = 2 if mode == "ffn" else 3
    f_all = nf * fc

    def up(j):
        return [_dot(hb, win_ref[:, b * f_all + j * fc:b * f_all + (j + 1) * fc]) for b in range(nbr)]

    def unit(j, p):
        cw = cw_ref[:, j * fc:(j + 1) * fc]
        if mode == "ffn":
            conv = _conv3(jnp.where(valid, p[0], 0.0), cw, tm)
            g = _silu(conv) * p[1][HALO:HALO + tm]
        else:
            conv = _conv3(jnp.where(valid, p[1] * p[2], 0.0), cw, tm)
            g = p[0][HALO:HALO + tm] * conv
        g_ref[:, j * fc:(j + 1) * fc] = g.astype(BF16)

    p = up(0)
    for j in range(nf):
        p_next = up(j + 1) if j + 1 < nf else None
        unit(j, p)
        p = p_next
    y = _dot(g_ref[...], wout_ref[...])
    o_ref[0] = x_main + gate * _rms(y, npost_ref[...])


def _glu_call(x, mod, npre, npost, win, cw, wout, *, mode, rows, tm, fc):
    bsz, t_len, d = x.shape
    f_all = wout.shape[0]
    assert f_all % fc == 0 and fc % LANES == 0
    nf = f_all // fc
    tm = min(tm, t_len)
    kern = functools.partial(_glu_kernel, mode=mode, tm=tm, nf=nf, fc=fc, rows=rows)
    return pl.pallas_call(
        kern,
        out_shape=jax.ShapeDtypeStruct(x.shape, F32),
        grid=(bsz, t_len // tm),
        in_specs=_halo_specs(tm, t_len, d) + [
            _mod_spec(mod),
            pl.BlockSpec((1, d), lambda b, t: (0, 0)),
            pl.BlockSpec((1, d), lambda b, t: (0, 0)),
            _resident(win.shape), _resident(cw.shape), _resident(wout.shape),
        ],
        out_specs=pl.BlockSpec((1, tm, d), lambda b, t: (b, t, 0)),
        scratch_shapes=[pltpu.VMEM((tm, f_all), BF16)],
        compiler_params=pltpu.CompilerParams(
            dimension_semantics=("parallel", "arbitrary"), vmem_limit_bytes=VMEM_LIMIT),
        name="glu_" + mode,
    )(x, x, x, mod, npre, npost, win, cw, wout)


def _proj_kernel(*refs, tm, chunk, q_side):
    (xm_ref, xp_ref, xn_ref, mod_ref, npre_ref, w_ref, cw_ref, alog_ref, dtb_ref,
     qn_ref, kn_ref, cos_ref, sin_ref) = refs[:13]
    outs = refs[13:]
    k_o, v_o, gc_o, gcT_o, ak_o, avT_o = outs[:6]
    if q_side:
        q_o, z_o, aq_o = outs[6:9]

    x_main = xm_ref[0]
    x_ext = jnp.concatenate([xp_ref[0], x_main, xn_ref[0]], axis=0)
    shift = mod_ref[0, 0:1, :]
    scale = mod_ref[0, 1:2, :]
    npre = npre_ref[...]
    hb_ext = (_rms(x_ext, npre) * (1.0 + scale) + shift).astype(BF16)
    hb = (_rms(x_main, npre) * (1.0 + scale) + shift).astype(BF16)
    valid = _valid_rows(tm)

    groups = [(PC_K, k_o, True, 1.0), (PC_V, v_o, False, 1.0)]
    if q_side:
        groups.append((PC_Q, q_o, True, DN_DK ** -0.5))
    for c0, o_ref, unit, mult in groups:
        p = jnp.where(valid, _dot(hb_ext, w_ref[:, c0:c0 + DN_W]), 0.0)
        a = _silu(_conv3(p, cw_ref[:, c0:c0 + DN_W], tm))
        if unit:
            for hd in range(DN_HEADS):
                seg = a[:, hd * DN_DK:(hd + 1) * DN_DK]
                inv = lax.rsqrt(jnp.sum(seg * seg, axis=-1, keepdims=True) + EPS)
                o_ref[0, :, hd * DN_DK:(hd + 1) * DN_DK] = (seg * (inv * mult)).astype(BF16)
        else:
            o_ref[0] = a.astype(BF16)

    lane = lax.broadcasted_iota(jnp.int32, (1, LANES), 1)
    half = (lane % ATT_HD) < (ATT_HD // 2)
    cos_t = cos_ref[...]
    sin_t = sin_ref[...]
    r_i = lax.broadcasted_iota(jnp.int32, (LANES, LANES), 0)
    c_i = lax.broadcasted_iota(jnp.int32, (LANES, LANES), 1)
    head_mean = jnp.where((r_i // ATT_HD) == (c_i // ATT_HD), 1.0 / ATT_HD, 0.0).astype(BF16)

    def head_norm_rope(xa, wn):
        ms = _dot((xa * xa).astype(BF16), head_mean)
        xa = xa * lax.rsqrt(ms + EPS) * wn
        swapped = jnp.where(half, pltpu.roll(xa, LANES - ATT_HD // 2, 1), pltpu.roll(xa, ATT_HD // 2, 1))
        return xa * cos_t + swapped * sin_t

    if q_side:
        z_o[0] = _dot(hb, w_ref[:, PC_Z:PC_AQ])
        pq = _dot(hb, w_ref[:, PC_AQ:PC_AK])
        for gq in range(ATT_Q_W // LANES):
            xa = head_norm_rope(pq[:, gq * LANES:(gq + 1) * LANES], qn_ref[...])
            aq_o[0, gq * LANES:(gq + 1) * LANES, :] = (xa * (ATT_HD ** -0.5 * LOG2E)).T.astype(BF16)

    pk = _dot(hb, w_ref[:, PC_AK:PC_END])
    ak = head_norm_rope(pk[:, :LANES], kn_ref[...])
    ak_o[0] = ak.astype(BF16)
    avT_o[0] = pk[:, LANES:2 * LANES].T.astype(BF16)

    pab = pk[:, 2 * LANES:]
    n_dh = DN_DIRS * DN_HEADS
    zab = pab + dtb_ref[...]
    softplus = jnp.maximum(zab, 0.0) + jnp.log(1.0 + jnp.exp(-jnp.abs(zab)))
    g = -jnp.exp(alog_ref[...]) * softplus
    beta = 1.0 / (1.0 + jnp.exp(-pab))
    g = jnp.where(lane < n_dh, g, 0.0)
    half_rows = min(tm, 2 * LANES)
    rr = lax.broadcasted_iota(jnp.int32, (half_rows, half_rows), 0)
    cc = lax.broadcasted_iota(jnp.int32, (half_rows, half_rows), 1)
    low = jnp.where(jnp.logical_and(rr // chunk == cc // chunk, cc <= rr), 1.0, 0.0).astype(BF16)
    pres = []
    for r0 in range(0, tm, half_rows):
        parts = jnp.concatenate(_split3(g[r0:r0 + half_rows]), axis=1)
        acc = _dot(low, parts)
        pres.append(acc[:, :LANES] + acc[:, LANES:2 * LANES] + acc[:, 2 * LANES:])
    pre = jnp.concatenate(pres, axis=0)
    total = jnp.concatenate([jnp.broadcast_to(pre[r + chunk - 1:r + chunk], (chunk, LANES))
                             for r in range(0, tm, chunk)], axis=0)
    suf = total - pre + g
    gc = jnp.where(lane < DN_HEADS, pre, jnp.where(lane < n_dh, suf, jnp.where(lane < 2 * n_dh, beta, 0.0)))
    gc_o[0] = gc
    gcT_o[0] = gc.T[0:2 * n_dh, :]


def _proj_call(x, mod, npre, w, cw, alog, dtb, qn, kn, cos_t, sin_t, *, q_side, tm, chunk):
    bsz, t_len, d = x.shape
    tm = min(tm, t_len)
    nt = t_len // tm
    f32_out = lambda n: jax.ShapeDtypeStruct((bsz, t_len, n), F32)
    bf16_out = lambda n: jax.ShapeDtypeStruct((bsz, t_len, n), BF16)
    row_spec = lambda n: pl.BlockSpec((1, tm, n), lambda b, t: (b, t, 0))
    out_shape = [bf16_out(DN_W), bf16_out(DN_W), f32_out(LANES),
                 jax.ShapeDtypeStruct((bsz, 2 * DN_DIRS * DN_HEADS, t_len), F32),
                 bf16_out(ATT_KV_W),
                 jax.ShapeDtypeStruct((bsz, ATT_KV_W, t_len), BF16)]
    out_specs = [row_spec(DN_W), row_spec(DN_W), row_spec(LANES),
                 pl.BlockSpec((1, 2 * DN_DIRS * DN_HEADS, tm), lambda b, t: (b, 0, t)),
                 row_spec(ATT_KV_W),
                 pl.BlockSpec((1, ATT_KV_W, tm), lambda b, t: (b, 0, t))]
    if q_side:
        out_shape += [bf16_out(DN_W), f32_out(DN_W), jax.ShapeDtypeStruct((bsz, ATT_Q_W, t_len), BF16)]
        out_specs += [row_spec(DN_W), row_spec(DN_W), pl.BlockSpec((1, ATT_Q_W, tm), lambda b, t: (b, 0, t))]
    vec = lambda n: pl.BlockSpec((1, n), lambda b, t: (0, 0))
    kern = functools.partial(_proj_kernel, tm=tm, chunk=chunk, q_side=q_side)
    return pl.pallas_call(
        kern,
        out_shape=out_shape,
        grid=(bsz, nt),
        in_specs=_halo_specs(tm, t_len, d) + [
            _mod_spec(mod), vec(d), _resident(w.shape), _resident(cw.shape),
            vec(LANES), vec(LANES), vec(LANES), vec(LANES),
            pl.BlockSpec((tm, LANES), lambda b, t: (t, 0)),
            pl.BlockSpec((tm, LANES), lambda b, t: (t, 0)),
        ],
        out_specs=out_specs,
        compiler_params=pltpu.CompilerParams(
            dimension_semantics=("parallel", "arbitrary"), vmem_limit_bytes=VMEM_LIMIT),
        name="proj_q" if q_side else "proj_kv",
    )(x, x, x, mod, npre, w, cw, alog, dtb, qn, kn, cos_t, sin_t)


def _dn_kernel(*refs, chunk, sup, n_ctx, n_lat, ctx_q, hp):
    c = chunk
    nb = sup // c
    n_in_ctx = 5 if ctx_q else 4
    ctx_in = refs[:n_in_ctx]
    lat_in = refs[n_in_ctx:n_in_ctx + 5]
    n_out = 2 if ctx_q else 1
    outs = refs[n_in_ctx + 5:n_in_ctx + 5 + n_out]
    u_s, w_s, kd_s, qg_s, in_s, eg_s = refs[n_in_ctx + 5 + n_out:]
    head0 = pl.program_id(1) * hp
    chains = [(hh, d) for hh in range(hp) for d in range(DN_DIRS)]
    n_dh = DN_DIRS * DN_HEADS

    ri = lax.broadcasted_iota(jnp.int32, (sup, sup), 0)
    ci = lax.broadcasted_iota(jnp.int32, (sup, sup), 1)
    same = (ri // c) == (ci // c)
    incl = [jnp.logical_and(same, ci <= ri), jnp.logical_and(same, ci >= ri)]
    strict = [jnp.logical_and(same, ci < ri), jnp.logical_and(same, ci > ri)]
    rw = lax.broadcasted_iota(jnp.int32, (c, sup), 0)
    cw = lax.broadcasted_iota(jnp.int32, (c, sup), 1)
    cl = cw % c
    lane_blk = cw // c
    eye_w = jnp.where(rw == cl, 1.0, 0.0)
    level_mask = []
    n = 1
    while n < c:
        level_mask.append(jnp.logical_and(rw // (2 * n) == cl // (2 * n), rw // n != cl // n))
        n *= 2
    lane = lax.broadcasted_iota(jnp.int32, (1, LANES), 1)
    sub = lax.broadcasted_iota(jnp.int32, (2 * n_dh, 1), 0)

    def pick_col(x, idx):
        return jnp.sum(jnp.where(lane == idx, x, 0.0), axis=-1, keepdims=True)

    def block_diag(xw):
        return jnp.where(same, jnp.concatenate([xw] * nb, axis=0), 0.0).astype(BF16)

    def prep(slot, src, pos, has_q):
        if has_q:
            k_ref, v_ref, gc_ref, gct_ref, q_ref = src
        else:
            k_ref, v_ref, gc_ref, gct_ref = src
        r0 = [pl.multiple_of(pos[d] * sup, sup) for d in range(DN_DIRS)]
        gcb = [gc_ref[0, pl.ds(r0[d], sup), :] for d in range(DN_DIRS)]
        gct = [gct_ref[0, :, pl.ds(r0[d], sup)] for d in range(DN_DIRS)]
        st = []
        for n, (hh, d) in enumerate(chains):
            hs = slice(hh * DN_DK, (hh + 1) * DN_DK)
            k16 = k_ref[0, pl.ds(r0[d], sup), hs]
            k = k16.astype(F32)
            v = v_ref[0, pl.ds(r0[d], sup), hs].astype(F32)
            q = q_ref[0, pl.ds(r0[d], sup), hs].astype(F32) if has_q else None
            col = d * DN_HEADS + head0 + hh
            gc_col = pick_col(gcb[d], col)
            beta = pick_col(gcb[d], n_dh + col)
            gc_row = jnp.sum(jnp.where(sub == col, gct[d], 0.0), axis=0, keepdims=True)
            decay = jnp.where(incl[d], jnp.exp(jnp.where(incl[d], gc_col - gc_row, 0.0)), 0.0)
            kbeta = k * beta
            if has_q:
                kk = _dot_nt(jnp.concatenate([kbeta, q], axis=0).astype(BF16), k16)
                low = jnp.where(strict[d], kk[:sup] * decay, 0.0)
                intra = kk[sup:] * decay
                for b in range(nb):
                    in_s[slot, n, b * c:(b + 1) * c, :] = intra[b * c:(b + 1) * c, b * c:(b + 1) * c].astype(BF16)
            else:
                low = jnp.where(strict[d], _dot_nt(kbeta.astype(BF16), k16) * decay, 0.0)
            low_w = jnp.where(lane_blk == 0, low[0:c], 0.0)
            for b in range(1, nb):
                low_w = low_w + jnp.where(lane_blk == b, low[b * c:(b + 1) * c], 0.0)
            egc = jnp.exp(gc_col)
            rhs = jnp.concatenate([v * beta, kbeta * egc], axis=1).astype(BF16)
            last = [b * c + (c - 1 if d == 0 else 0) for b in range(nb)]
            g_last = jnp.concatenate([jnp.broadcast_to(gc_col[i:i + 1], (c, 1)) for i in last], axis=0)
            kd_s[slot, n] = (k * jnp.exp(g_last - gc_col)).astype(BF16)
            for b, i in enumerate(last):
                eg_s[slot, n, b] = jnp.broadcast_to(jnp.exp(gc_col[i:i + 1]), (HALO, LANES))
            if has_q:
                qg_s[slot, n] = (q * egc).astype(BF16)
            st.append((low_w, rhs))
            yield

        tw = [eye_w - jnp.where(level_mask[0], s[0], 0.0) for s in st]
        for lm in level_mask[1:]:
            et = [_dot(jnp.where(lm, s[0], 0.0).astype(BF16), block_diag(t)) for s, t in zip(st, tw)]
            yield
            tw = [t - _dot(t.astype(BF16), block_diag(e)) for t, e in zip(tw, et)]
            yield

        for n, (s, t) in enumerate(zip(st, tw)):
            uw = _dot(block_diag(t), s[1])
            u_s[slot, n] = uw[:, :DN_DK]
            w_s[slot, n] = uw[:, DN_DK:].astype(BF16)
            yield

    def scan_step(slot, t, states, rows_out, o_ref, has_q):
        new = []
        rows = [slice(t * c, (t + 1) * c), slice((nb - 1 - t) * c, (nb - t) * c)]
        blk = [t, nb - 1 - t]
        s16 = [s.astype(BF16) for s in states]
        if has_q:
            ws = [_dot(jnp.concatenate([w_s[slot, n, rows[d], :], qg_s[slot, n, rows[d], :]], axis=0), s16[n])
                  for n, (hh, d) in enumerate(chains)]
            v16 = [(u_s[slot, n, rows[d], :] - ws[n][:c]).astype(BF16) for n, (hh, d) in enumerate(chains)]
            for n, (hh, d) in enumerate(chains):
                o = ws[n][c:] + _dot(in_s[slot, n, rows[d], :], v16[n])
                dst = pl.ds(pl.multiple_of(rows_out[d] + blk[d] * c, c), c)
                o_ref[0, dst, hh * DN_DK:(hh + 1) * DN_DK] += o
        else:
            v16 = [(u_s[slot, n, rows[d], :] - _dot(w_s[slot, n, rows[d], :], s16[n])).astype(BF16)
                   for n, (hh, d) in enumerate(chains)]
        for n, (hh, d) in enumerate(chains):
            new.append(states[n] * eg_s[slot, n, blk[d], 0:1, :] + _dot_tn(kd_s[slot, n, rows[d], :], v16[n]))
        return tuple(new)

    def run(prep_gen, slot, states, rows_out, o_ref, has_q):
        holder = [states]
        todo = list(range(nb)) if slot is not None else []
        if prep_gen is not None:
            n_stage = 2 * len(chains) + 2 * (len(level_mask) - 1)
            every = max(n_stage // (nb + 1), 1)
            for i, _ in enumerate(prep_gen):
                if todo and (i + 1) % every == 0:
                    holder[0] = scan_step(slot, todo.pop(0), holder[0], rows_out, o_ref, has_q)
        for t in todo:
            holder[0] = scan_step(slot, t, holder[0], rows_out, o_ref, has_q)
        return holder[0]

    for o_ref in outs:
        o_ref[...] = jnp.zeros_like(o_ref)

    zero = jnp.zeros((DN_DK, DN_DK), F32)
    states = tuple(zero for _ in chains)

    def ctx_pos(p):
        return (p, n_ctx - 1 - p)

    def lat_pos(p):
        return (p, n_lat - 1 - p)

    def rows_of(pos):
        return [pos[0] * sup, pos[1] * sup]

    o_ctx = outs[1] if ctx_q else None
    run(prep(0, ctx_in, ctx_pos(0), ctx_q), None, states, None, None, False)
    for p in range(n_ctx):
        if p + 1 < n_ctx:
            nxt = prep((p + 1) % 2, ctx_in, ctx_pos(p + 1), ctx_q)
        else:
            nxt = prep((p + 1) % 2, lat_in, lat_pos(0), True)
        states = run(nxt, p % 2, states, rows_of(ctx_pos(p)), o_ctx, ctx_q)

    def lat_step(p, slot, states, last=False):
        nxt = None if last else prep(1 - slot, lat_in, lat_pos(p + 1), True)
        return run(nxt, slot, states, rows_of(lat_pos(p)), outs[0], True)

    s0 = n_ctx % 2
    n_loop = (n_lat - 1) // 2

    def lat_body(k, states):
        states = lat_step(2 * k, s0, states)
        return lat_step(2 * k + 1, 1 - s0, states)

    states = lax.fori_loop(0, n_loop, lat_body, states)
    for p in range(2 * n_loop, n_lat):
        states = lat_step(p, (s0 + p) % 2, states, last=p == n_lat - 1)


def _dn_call(ctx_in, lat_in, *, chunk, sup, hp, ctx_q):
    bsz, t_lat, _ = lat_in[0].shape
    t_ctx = ctx_in[0].shape[1]
    assert t_ctx % sup == 0 and t_lat % sup == 0 and sup % chunk == 0 and DN_HEADS % hp == 0
    n_ctx, n_lat = t_ctx // sup, t_lat // sup
    n_rows = 2 * DN_DIRS * DN_HEADS
    nb = sup // chunk

    def specs(t_len, has_q):
        head_blk = pl.BlockSpec((1, t_len, hp * DN_DK), lambda b, h: (b, 0, h))
        sp = [head_blk, head_blk,
              pl.BlockSpec((1, t_len, LANES), lambda b, h: (b, 0, 0)),
              pl.BlockSpec((1, n_rows, t_len), lambda b, h: (b, 0, 0))]
        if has_q:
            sp.append(head_blk)
        return sp

    out_shape = [jax.ShapeDtypeStruct((bsz, t_lat, DN_W), F32)]
    out_specs = [pl.BlockSpec((1, t_lat, hp * DN_DK), lambda b, h: (b, 0, h))]
    if ctx_q:
        out_shape.append(jax.ShapeDtypeStruct((bsz, t_ctx, DN_W), F32))
        out_specs.append(pl.BlockSpec((1, t_ctx, hp * DN_DK), lambda b, h: (b, 0, h)))
    nch = hp * DN_DIRS
    scratch = [
        pltpu.VMEM((2, nch, sup, DN_DK), F32),
        pltpu.VMEM((2, nch, sup, DN_DK), BF16),
        pltpu.VMEM((2, nch, sup, DN_DK), BF16),
        pltpu.VMEM((2, nch, sup, DN_DK), BF16),
        pltpu.VMEM((2, nch, sup, chunk), BF16),
        pltpu.VMEM((2, nch, nb, HALO, LANES), F32),
    ]
    kern = functools.partial(_dn_kernel, chunk=chunk, sup=sup, n_ctx=n_ctx, n_lat=n_lat, ctx_q=ctx_q, hp=hp)
    return pl.pallas_call(
        kern,
        out_shape=out_shape,
        grid=(bsz, DN_HEADS // hp),
        in_specs=specs(t_ctx, ctx_q) + specs(t_lat, True),
        out_specs=out_specs,
        scratch_shapes=scratch,
        compiler_params=pltpu.CompilerParams(
            dimension_semantics=("parallel", "arbitrary"), vmem_limit_bytes=VMEM_LIMIT),
        name="dn_q" if ctx_q else "dn_kv",
    )(*ctx_in, *lat_in)


def _attn_kernel(*refs, n_sets):
    q_ref = refs[0]
    kv = refs[1:1 + 2 * n_sets]
    o_ref = refs[1 + 2 * n_sets]
    slab = 64

    def fold(x, op):
        acc = x[0:slab]
        for r in range(slab, x.shape[0], slab):
            acc = op(acc, x[r:r + slab])
        return acc

    def head_scores(h):
        g = h // ATT_GROUPS
        q_t = q_ref[0, h * ATT_HD:(h + 1) * ATT_HD, :]
        zero = jnp.zeros_like(q_t)
        q_pad = jnp.concatenate([q_t, zero] if g == 0 else [zero, q_t], axis=0)
        return [_dot(kv[2 * i][0], q_pad) for i in range(n_sets)]

    def head_out(h, ps, den):
        g = h // ATT_GROUPS
        acc = _dot(kv[1][0, g * ATT_HD:(g + 1) * ATT_HD, :], ps[0])
        for i in range(1, n_sets):
            acc = acc + _dot(kv[2 * i + 1][0, g * ATT_HD:(g + 1) * ATT_HD, :], ps[i])
        return acc / den

    outs = []
    scores = head_scores(0)
    prev = None
    for h in range(ATT_HEADS):
        nxt = head_scores(h + 1) if h + 1 < ATT_HEADS else None
        if prev is not None:
            outs.append(head_out(h - 1, *prev))
        m = fold(scores[0], jnp.maximum)
        for s in scores[1:]:
            m = jnp.maximum(m, fold(s, jnp.maximum))
        m = m.max(axis=0, keepdims=True)
        ps = [jnp.exp2(s - m) for s in scores]
        den = fold(ps[0], jnp.add)
        for p in ps[1:]:
            den = den + fold(p, jnp.add)
        den = den.sum(axis=0, keepdims=True)
        prev = ([p.astype(BF16) for p in ps], den)
        scores = nxt
    outs.append(head_out(ATT_HEADS - 1, *prev))
    o_ref[0] = jnp.concatenate(outs, axis=0).T.astype(BF16)


def _attn_call(q_t, kv_sets, *, tq):
    bsz, _, t_len = q_t.shape
    tq = min(tq, t_len)
    in_specs = [pl.BlockSpec((1, ATT_Q_W, tq), lambda b, t: (b, 0, t))]
    args = [q_t]
    for k, v_t in kv_sets:
        s_len = k.shape[1]
        in_specs += [pl.BlockSpec((1, s_len, ATT_KV_W), lambda b, t: (b, 0, 0)),
                     pl.BlockSpec((1, ATT_KV_W, s_len), lambda b, t: (b, 0, 0))]
        args += [k, v_t]
    return pl.pallas_call(
        functools.partial(_attn_kernel, n_sets=len(kv_sets)),
        out_shape=jax.ShapeDtypeStruct((bsz, t_len, ATT_Q_W), BF16),
        grid=(bsz, t_len // tq),
        in_specs=in_specs,
        out_specs=pl.BlockSpec((1, tq, ATT_Q_W), lambda b, t: (b, t, 0)),
        compiler_params=pltpu.CompilerParams(
            dimension_semantics=("parallel", "arbitrary"), vmem_limit_bytes=VMEM_LIMIT),
        name="attn%d" % len(kv_sets),
    )(*args)


def _mixout_kernel(x_ref, o_ref, z_ref, b_ref, mod_ref, dnn_ref, npost_ref, w_ref, out_ref):
    gate = mod_ref[0, 2:3, :]
    o = o_ref[0]
    z = z_ref[0]
    parts = []
    for hd in range(DN_HEADS):
        seg = o[:, hd * DN_DK:(hd + 1) * DN_DK]
        parts.append(_rms(seg, dnn_ref[...]) * _silu(z[:, hd * DN_DK:(hd + 1) * DN_DK]))
    a = jnp.concatenate(parts, axis=1).astype(BF16)
    y = _dot(a, w_ref[0:DN_W, :]) + _dot(b_ref[0], w_ref[DN_W:, :])
    out_ref[0] = x_ref[0] + gate * _rms(y, npost_ref[...])


def _mixout_call(x, o_dn, z, b_att, mod, dnn, npost, w_out, *, tm):
    bsz, t_len, d = x.shape
    tm = min(tm, t_len)
    row = lambda n: pl.BlockSpec((1, tm, n), lambda b, t: (b, t, 0))
    return pl.pallas_call(
        _mixout_kernel,
        out_shape=jax.ShapeDtypeStruct(x.shape, F32),
        grid=(bsz, t_len // tm),
        in_specs=[row(d), row(DN_W), row(DN_W), row(ATT_Q_W), _mod_spec(mod),
                  pl.BlockSpec((1, DN_DK), lambda b, t: (0, 0)),
                  pl.BlockSpec((1, d), lambda b, t: (0, 0)),
                  _resident(w_out.shape)],
        out_specs=row(d),
        compiler_params=pltpu.CompilerParams(
            dimension_semantics=("parallel", "arbitrary"), vmem_limit_bytes=VMEM_LIMIT),
        name="mixout",
    )(x, o_dn, z, b_att, mod, dnn, npost, w_out)


def _rope_tables(n):
    half = ATT_HD // 2
    r = (jnp.arange(n) // GRID_W).astype(F32)
    col = (jnp.arange(n) % GRID_W).astype(F32)
    inv = ROPE_THETA ** (-jnp.arange(0, half, 2, dtype=F32) / half)
    ang = jnp.concatenate([r[:, None] * inv, col[:, None] * inv], axis=-1)
    cos, sin = jnp.cos(ang), jnp.sin(ang)
    cos_t = jnp.tile(jnp.concatenate([cos, cos], axis=-1), (1, LANES // ATT_HD))
    sin_t = jnp.tile(jnp.concatenate([-sin, sin], axis=-1), (1, LANES // ATT_HD))
    return cos_t, sin_t


def _even_weights(w_in, q_norm, k_norm, a_log, dt_bias):
    d = w_in.shape[0]
    n_ab = 2 * DN_DIRS * DN_HEADS
    off_ab = 2 * DN_W
    off_ak = off_ab + n_ab
    off_av = off_ak + ATT_KV_W
    off_q = off_av + ATT_KV_W
    off_z = off_q + DN_W
    off_aq = off_z + DN_W

    def deinterleave(a, heads):
        lead = a.shape[:-1]
        a = a.reshape(lead + (heads, ATT_HD // 2, 2))
        return jnp.swapaxes(a, -1, -2).reshape(lead + (heads * ATT_HD,))

    w = jnp.concatenate([
        w_in[:, :off_ab], w_in[:, off_q:off_aq],
        deinterleave(w_in[:, off_aq:off_aq + ATT_Q_W], ATT_HEADS),
        deinterleave(w_in[:, off_ak:off_av], ATT_KV_HEADS),
        w_in[:, off_av:off_q], w_in[:, off_ab:off_ak],
        jnp.zeros((d, PC_END - PC_AB - n_ab), w_in.dtype)], axis=1).astype(BF16)
    tile2 = lambda v: jnp.tile(deinterleave(v, 1), LANES // ATT_HD).reshape(1, LANES)
    pad = lambda v: jnp.pad(v.reshape(-1), (0, LANES - v.size)).reshape(1, LANES)
    return w, tile2(q_norm), tile2(k_norm), pad(a_log), pad(dt_bias)


def kernel(x, c, ctx, c_ctx, mod_w, mod_b, norm_mix_pre, norm_mix_post, norm_ffn_pre, norm_ffn_post,
           hyb_w_in, hyb_w_out, dn_conv_w, dn_a_log, dn_dt_bias, dn_out_norm, att_q_norm, att_k_norm,
           sc_w_in, sc_conv_w, sc_w_out, ffn_w_up, ffn_conv_w, ffn_w_down):
    bsz, seq, d = x.shape
    depth = mod_w.shape[0]
    ctx_len = ctx.shape[1]
    fc, tm_glu, tm_proj, tm_out, tq = FC_GLU, TM_GLU, TM_PROJ, TM_OUT, TQ_ATTN

    rows = ((bsz + 1 + HALO - 1) // HALO) * HALO
    s_pad = jnp.zeros((rows, d), F32).at[:bsz].set(c).at[bsz].set(c_ctx)
    mod_all = _mod_call(s_pad, mod_w, mod_b).reshape(depth, rows, 6, d)

    cos_l, sin_l = _rope_tables(seq)
    cos_c, sin_c = jnp.ones((ctx_len, LANES), F32), jnp.zeros((ctx_len, LANES), F32)
    vec = lambda v: v.reshape(1, -1)

    for layer in range(depth):
        even = layer % 2 == 0
        ctx_live = any(j % 2 == 0 for j in range(layer + 1, depth))
        mod_l = mod_all[layer, :bsz]
        mod_c = mod_all[layer, bsz:bsz + 1]
        npre, npost = vec(norm_mix_pre[layer]), vec(norm_mix_post[layer])
        if even:
            e = layer // 2
            w, qn, kn, alog, dtb = _even_weights(hyb_w_in[e], att_q_norm[e], att_k_norm[e],
                                                 dn_a_log[e], dn_dt_bias[e])
            cw = dn_conv_w[e]
            w_out = hyb_w_out[e].astype(BF16)
            dnn = vec(dn_out_norm[e])
            proj = functools.partial(_proj_call, npre=npre, w=w, cw=cw, alog=alog, dtb=dtb, qn=qn, kn=kn,
                                     tm=tm_proj, chunk=DN_CHUNK)
            pl_ = proj(x, mod_l, cos_t=cos_l, sin_t=sin_l, q_side=True)
            pc_ = proj(ctx, mod_c, cos_t=cos_c, sin_t=sin_c, q_side=ctx_live)
            k_l, v_l, gc_l, gct_l, ak_l, avt_l, q_l, z_l, aq_l = pl_
            k_c, v_c, gc_c, gct_c, ak_c, avt_c = pc_[:6]
            lat_in = (k_l, v_l, gc_l, gct_l, q_l)
            ctx_in = (k_c, v_c, gc_c, gct_c) + ((pc_[6],) if ctx_live else ())
            o_dn = _dn_call(ctx_in, lat_in, chunk=DN_CHUNK, sup=DN_SUPER, hp=DN_HEADS_PER_STEP, ctx_q=ctx_live)
            b_lat = _attn_call(aq_l, [(ak_c, avt_c), (ak_l, avt_l)], tq=tq)
            x_new = _mixout_call(x, o_dn[0], z_l, b_lat, mod_l, dnn, npost, w_out, tm=tm_out)
            if ctx_live:
                b_ctx = _attn_call(pc_[8], [(ak_c, avt_c)], tq=tq)
                ctx = _mixout_call(ctx, o_dn[1], pc_[7], b_ctx, mod_c, dnn, npost, w_out, tm=tm_out)
            x = x_new
        else:
            o = layer // 2
            sc = functools.partial(_glu_call, npre=npre, npost=npost, win=sc_w_in[o].astype(BF16), cw=sc_conv_w[o],
                                   wout=sc_w_out[o].astype(BF16), mode="sc", rows=(0, 1, 2), tm=TM_SC, fc=fc)
            x = sc(x, mod_l)
            if ctx_live:
                ctx = sc(ctx, mod_c)
        ffn = functools.partial(_glu_call, npre=vec(norm_ffn_pre[layer]), npost=vec(norm_ffn_post[layer]),
                                win=ffn_w_up[layer].astype(BF16), cw=ffn_conv_w[layer],
                                wout=ffn_w_down[layer].astype(BF16), mode="ffn", rows=(3, 4, 5), tm=tm_glu, fc=fc)
        x = ffn(x, mod_l)
        if ctx_live:
            ctx = ffn(ctx, mod_c)
    return x
```
